```python
import math
import jax, jax.numpy as jnp
from jax import lax
import numpy as np

D_MODEL = 1024
BATCH = 16
SEQ = 2048
DEPTH = 2
DEC_BATCH = 32
DEC_SEQ = 4
PAST_LEN = 16384
PAGE_SIZE = 128

MIX_W = D_MODEL
A_W = D_MODEL // 4
A_HEADS = 4
A_DK = A_W // A_HEADS
A_DV = A_W // A_HEADS
A_CHUNK = 64
B_W = 3 * D_MODEL // 8
B_HEADS = 6
B_HD = B_W // B_HEADS
IDX_HEADS = 8
IDX_D = 64
TOPK_MAX = 256
C_W = MIX_W - A_W - B_W
C_HEADS = 4
C_DV = C_W // C_HEADS
C_QK = C_DV // 2
ROPE_THETA = 500000.0
Q_BLOCK = 128
EPS = 1e-6
NEG_BIG = -1e30
IN_SIZES = (A_W, A_W, A_W, A_W, B_W, B_W, B_W, B_W, IDX_HEADS * IDX_D, IDX_D, IDX_HEADS, C_W, C_W, C_W, C_W)
N_IN = 4 * A_W + 4 * B_W + IDX_HEADS * IDX_D + IDX_D + IDX_HEADS + 4 * C_W

kernel_name = 'hymba_hgrn2_dsa_diffattn_step'


def rms_norm(x, w):
    xf = x.astype(jnp.float32)
    y = xf * lax.rsqrt(jnp.mean(xf * xf, axis=-1, keepdims=True) + EPS)
    return (y * w.astype(jnp.float32)).astype(x.dtype)


def rope_partial(x, pos):
    d = x.shape[-1]
    rot = d // 4
    half = rot // 2
    inv = jnp.power(ROPE_THETA, -2.0 * jnp.arange(half, dtype=jnp.float32) / rot)
    ang = pos.astype(jnp.float32)[:, None] * inv[None, :]
    ang = ang.reshape((pos.shape[0],) + (1,) * (x.ndim - 3) + (half,))
    cos, sin = jnp.cos(ang), jnp.sin(ang)
    xr = x[..., :rot].astype(jnp.float32)
    x1, x2 = xr[..., :half], xr[..., half:]
    rotated = jnp.concatenate([x1 * cos - x2 * sin, x2 * cos + x1 * sin], axis=-1).astype(x.dtype)
    return jnp.concatenate([rotated, x[..., rot:]], axis=-1)


def mixer_inputs(x, pos, l, norm_w, w_in, lb_all, dsa_qnorm_w, dsa_knorm_w, diff_qnorm_w, diff_knorm_w):
    Bn, T, _ = x.shape
    h = rms_norm(x, norm_w[l])
    z = jnp.einsum('btd,dn->btn', h, w_in[l])
    splits = [int(v) for v in np.cumsum(IN_SIZES)[:-1]]
    (a_q, a_f, a_i, a_g, b_q, b_k, b_v, b_g, i_q, i_k, i_w, c_q, c_k, c_v, c_g) = jnp.split(z, splits, axis=-1)
    lb = lb_all[l].reshape(A_HEADS, A_DK)
    fz = a_f.reshape(Bn, T, A_HEADS, A_DK).astype(jnp.float32)
    f = lb + (1.0 - lb) * jax.nn.sigmoid(fz)
    log_f = jnp.log(f)
    k_a = (1.0 - lb) * jax.nn.sigmoid(-fz)
    return {
        'a_q': a_q.reshape(Bn, T, A_HEADS, A_DK), 'a_k': k_a, 'a_logf': log_f,
        'a_v': a_i.reshape(Bn, T, A_HEADS, A_DV), 'a_g': a_g,
        'b_q': rope_partial(rms_norm(b_q.reshape(Bn, T, B_HEADS, B_HD), dsa_qnorm_w[l]), pos),
        'b_k': rope_partial(rms_norm(b_k.reshape(Bn, T, B_HEADS, B_HD), dsa_knorm_w[l]), pos),
        'b_v': b_v.reshape(Bn, T, B_HEADS, B_HD), 'b_g': b_g,
        'i_q': rope_partial(i_q.reshape(Bn, T, IDX_HEADS, IDX_D), pos),
        'i_k': rope_partial(i_k, pos),
        'i_w': i_w * (IDX_HEADS ** -0.5),
        'c_q': rope_partial(rms_norm(c_q.reshape(Bn, T, C_HEADS, 2, C_QK), diff_qnorm_w[l]), pos),
        'c_k': rope_partial(rms_norm(c_k.reshape(Bn, T, C_HEADS, 2, C_QK), diff_knorm_w[l]), pos),
        'c_v': c_v.reshape(Bn, T, C_HEADS, C_DV), 'c_g': c_g,
    }


def hgrn2_chunked(q, k, v, log_f, s0):
    Bn, T, H, DK = q.shape
    DV = v.shape[-1]
    C = A_CHUNK if T % A_CHUNK == 0 else T
    n = T // C

    def blocks(a):
        return a.astype(jnp.float32).reshape(Bn, n, C, H, a.shape[-1]).transpose(1, 0, 3, 2, 4)

    causal = jnp.tril(jnp.ones((C, C), dtype=bool))[:, :, None]

    def step(S, chunk):
        qc, kc, vc, lfc = chunk
        b = jnp.cumsum(lfc, axis=2)
        b_last = b[:, :, -1:, :]
        o_inter = jnp.einsum('bhtd,bhde->bhte', qc * jnp.exp(b), S)
        rel = jnp.where(causal, b[:, :, :, None, :] - b[:, :, None, :, :], 0.0)
        decay = jnp.where(causal, jnp.exp(rel), 0.0)
        scores = jnp.einsum('bhtd,bhsd,bhtsd->bhts', qc, kc, decay)
        o_intra = jnp.einsum('bhts,bhse->bhte', scores, vc)
        S_new = jnp.exp(b_last[:, :, 0, :])[..., None] * S + jnp.einsum('bhsd,bhse->bhde', kc * jnp.exp(b_last - b), vc)
        return S_new, o_inter + o_intra

    S_fin, o = lax.scan(step, s0.astype(jnp.float32), (blocks(q), blocks(k), blocks(v), blocks(log_f)))
    o = o.transpose(1, 0, 3, 2, 4).reshape(Bn, T, H, DV)
    return o.astype(v.dtype), S_fin.astype(s0.dtype)


def dsa_topk(i_q, i_w, i_k, q_pos, n_keep):
    L = i_k.shape[-2]
    s = jnp.einsum('...thd,...ld->...thl', i_q, i_k).astype(jnp.float32) * (IDX_D ** -0.5)
    score = jnp.einsum('...th,...thl->...tl', i_w.astype(jnp.float32), jax.nn.relu(s))
    visible = jnp.arange(L)[None, :] <= q_pos[:, None]
    score = jnp.where(visible, score, NEG_BIG)
    _, idx = lax.top_k(score, n_keep)
    valid = idx <= q_pos[:, None]
    return idx, valid


def sparse_attn(q, k_sel, v_sel, valid):
    s = jnp.einsum('...thd,...tkhd->...thk', q, k_sel).astype(jnp.float32) * (B_HD ** -0.5)
    s = jnp.where(valid[..., None, :], s, NEG_BIG)
    p = jax.nn.softmax(s, axis=-1).astype(v_sel.dtype)
    return jnp.einsum('...thk,...tkhd->...thd', p, v_sel)


def dsa_prompt(q, k, v, i_q, i_w, i_k):
    Bn, S = q.shape[:2]
    qb = min(Q_BLOCK, S)
    nb = S // qb
    n_keep = min(TOPK_MAX, S // 4)
    pos_blocks = jnp.arange(S, dtype=jnp.int32).reshape(nb, qb)

    def one_seq(args):
        q_s, k_s, v_s, iq_s, iw_s, ik_s = args

        def one_block(blk):
            q_b, iq_b, iw_b, p_b = blk
            idx, valid = dsa_topk(iq_b, iw_b, ik_s, p_b, n_keep)
            return sparse_attn(q_b, k_s[idx], v_s[idx], valid)

        out = lax.map(one_block, (q_s.reshape(nb, qb, B_HEADS, B_HD), iq_s.reshape(nb, qb, IDX_HEADS, IDX_D),
                                  iw_s.reshape(nb, qb, IDX_HEADS), pos_blocks))
        return out.reshape(S, B_HEADS, B_HD)

    return lax.map(one_seq, (q, k, v, i_q, i_w, i_k))


def dsa_sample(q, k_new, v_new, i_q, i_w, ik_new, pool_k, pool_v, pool_ik, page_table, q_pos, l):
    Bd, T = q.shape[:2]
    past = page_table.shape[1] * PAGE_SIZE
    L = past + T
    n_keep = min(TOPK_MAX, L // 4)
    ik_all = jnp.concatenate([pool_ik[l, page_table].reshape(Bd, past, IDX_D), ik_new], axis=1)
    idx, valid = dsa_topk(i_q, i_w, ik_all, q_pos, n_keep)
    from_new = (idx >= past)[..., None, None]
    idx_past = jnp.minimum(idx, past - 1)
    b_ix = jnp.arange(Bd)[:, None, None]
    phys = page_table[b_ix, idx_past // PAGE_SIZE]
    off = idx_past % PAGE_SIZE
    new_ix = jnp.clip(idx - past, 0, T - 1)
    k_sel = jnp.where(from_new, k_new[b_ix, new_ix], pool_k[l, phys, off])
    v_sel = jnp.where(from_new, v_new[b_ix, new_ix], pool_v[l, phys, off])
    return sparse_attn(q, k_sel, v_sel, valid)


def diff_attn(q, k, v, q_pos, lam):
    L = k.shape[-4]
    s = jnp.einsum('...thcd,...lhcd->...hctl', q, k).astype(jnp.float32) * (C_QK ** -0.5)
    visible = jnp.arange(L)[None, :] <= q_pos[:, None]
    p = jax.nn.softmax(jnp.where(visible, s, NEG_BIG), axis=-1)
    a = p[..., 0, :, :] - lam * p[..., 1, :, :]
    return jnp.einsum('...htl,...lhe->...the', a.astype(v.dtype), v)


def diff_prompt(q, k, v, lam):
    Bn, S = q.shape[:2]
    qb = min(Q_BLOCK, S)
    nb = S // qb
    q_blocks = jnp.moveaxis(q.reshape(Bn, nb, qb, C_HEADS, 2, C_QK), 1, 0)
    pos_blocks = jnp.arange(S, dtype=jnp.int32).reshape(nb, qb)
    out = lax.map(lambda blk: diff_attn(blk[0], k, v, blk[1], lam), (q_blocks, pos_blocks))
    return jnp.moveaxis(out, 0, 1).reshape(Bn, S, C_HEADS, C_DV)


def diff_sample(q, k_new, v_new, pool_k, pool_v, page_table, q_pos, lam, l):
    past = page_table.shape[1] * PAGE_SIZE

    def one_seq(args):
        q_s, kn, vn, pt = args
        k_all = jnp.concatenate([pool_k[l, pt].reshape(past, C_HEADS, 2, C_QK), kn], axis=0)
        v_all = jnp.concatenate([pool_v[l, pt].reshape(past, C_HEADS, C_DV), vn], axis=0)
        return diff_attn(q_s, k_all, v_all, q_pos, lam)

    return lax.map(one_seq, (q, k_new, v_new, page_table))


def mixer_output(x, inp, o_a, o_b, o_c, l, lam_init, hgrn_onorm_w, diff_subln_w, w_out):
    Bn, T, _ = x.shape
    g_a = rms_norm(o_a, hgrn_onorm_w[l]).reshape(Bn, T, A_W) * jax.nn.silu(inp['a_g'])
    g_b = o_b.reshape(Bn, T, B_W) * jax.nn.silu(inp['b_g'])
    g_c = (rms_norm(o_c, diff_subln_w[l]) * (1.0 - lam_init)).reshape(Bn, T, C_W) * jax.nn.silu(inp['c_g'])
    o = jnp.concatenate([g_a, g_b, g_c], axis=-1)
    return x + jnp.einsum('btm,md->btd', o, w_out[l])


def setup_inputs(seed: int = 0) -> dict:
    key = jax.random.key(seed)
    ks = jax.random.split(key, 24)
    n_pages = PAST_LEN // PAGE_SIZE
    n_pool = (DEC_BATCH * n_pages * 5) // 4
    nrm = jax.random.normal
    page_table = jax.random.permutation(ks[8], n_pool)[:DEC_BATCH * n_pages].reshape(DEC_BATCH, n_pages).astype(jnp.int32)
    return {
        'x_prompt': nrm(ks[0], (BATCH, SEQ, D_MODEL), jnp.float32),
        'x_sample': nrm(ks[1], (DEC_BATCH, DEC_SEQ, D_MODEL), jnp.float32),
        'cache_dsa_k': nrm(ks[2], (DEPTH, n_pool, PAGE_SIZE, B_HEADS, B_HD), jnp.float32),
        'cache_dsa_v': nrm(ks[3], (DEPTH, n_pool, PAGE_SIZE, B_HEADS, B_HD), jnp.float32),
        'cache_idx_k': nrm(ks[4], (DEPTH, n_pool, PAGE_SIZE, IDX_D), jnp.float32),
        'cache_diff_k': nrm(ks[5], (DEPTH, n_pool, PAGE_SIZE, C_HEADS, 2, C_QK), jnp.float32),
        'cache_diff_v': nrm(ks[6], (DEPTH, n_pool, PAGE_SIZE, C_HEADS, C_DV), jnp.float32),
        'state_hgrn': 0.5 * nrm(ks[7], (DEPTH, DEC_BATCH, A_HEADS, A_DK, A_DV), jnp.float32),
        'page_table': page_table,
        'norm_w': 1.0 + 0.02 * nrm(ks[9], (DEPTH, D_MODEL), jnp.float32),
        'w_in': nrm(ks[10], (DEPTH, D_MODEL, N_IN), jnp.float32) * (D_MODEL ** -0.5),
        'w_out': nrm(ks[11], (DEPTH, MIX_W, D_MODEL), jnp.float32) * (MIX_W ** -0.5),
        'hgrn_lb_logits': 0.1 * nrm(ks[12], (DEPTH, A_W), jnp.float32),
        'hgrn_onorm_w': 1.0 + 0.02 * nrm(ks[13], (DEPTH, A_DV), jnp.float32),
        'dsa_qnorm_w': 1.0 + 0.02 * nrm(ks[14], (DEPTH, B_HD), jnp.float32),
        'dsa_knorm_w': 1.0 + 0.02 * nrm(ks[15], (DEPTH, B_HD), jnp.float32),
        'diff_qnorm_w': 1.0 + 0.02 * nrm(ks[16], (DEPTH, C_QK), jnp.float32),
        'diff_knorm_w': 1.0 + 0.02 * nrm(ks[17], (DEPTH, C_QK), jnp.float32),
        'diff_lambda': 0.1 * nrm(ks[18], (DEPTH, 4, C_QK), jnp.float32),
        'diff_subln_w': 1.0 + 0.02 * nrm(ks[19], (DEPTH, C_DV), jnp.float32),
    }


def reference(x_prompt, x_sample, cache_dsa_k, cache_dsa_v, cache_idx_k, cache_diff_k, cache_diff_v, state_hgrn,
              page_table, norm_w, w_in, w_out, hgrn_lb_logits, hgrn_onorm_w, dsa_qnorm_w, dsa_knorm_w,
              diff_qnorm_w, diff_knorm_w, diff_lambda, diff_subln_w):
    f32 = jnp.float32
    lb_soft = jax.nn.softmax(hgrn_lb_logits.astype(f32), axis=0)
    lb_all = jnp.cumsum(lb_soft, axis=0) - lb_soft[0:1]
    S = x_prompt.shape[1]
    T = x_sample.shape[1]
    past = page_table.shape[1] * PAGE_SIZE
    pos_p = jnp.arange(S, dtype=jnp.int32)
    pos_s = past + jnp.arange(T, dtype=jnp.int32)
    xp, xs = x_prompt, x_sample
    p_bk, p_bv, p_ik, p_ck, p_cv, p_st = [], [], [], [], [], []
    s_bk, s_bv, s_ik, s_ck, s_cv, s_st = [], [], [], [], [], []
    for l in range(DEPTH):
        lam_init = 0.8 - 0.6 * math.exp(-0.3 * l)
        dl = diff_lambda[l].astype(f32)
        lam = jnp.exp(jnp.sum(dl[0] * dl[1])) - jnp.exp(jnp.sum(dl[2] * dl[3])) + lam_init
        ip = mixer_inputs(xp, pos_p, l, norm_w, w_in, lb_all, dsa_qnorm_w, dsa_knorm_w, diff_qnorm_w, diff_knorm_w)
        s0 = jnp.zeros((xp.shape[0], A_HEADS, A_DK, A_DV), xp.dtype)
        oa, st_p = hgrn2_chunked(ip['a_q'], ip['a_k'], ip['a_v'], ip['a_logf'], s0)
        ob = dsa_prompt(ip['b_q'], ip['b_k'], ip['b_v'], ip['i_q'], ip['i_w'], ip['i_k'])
        oc = diff_prompt(ip['c_q'], ip['c_k'], ip['c_v'], lam)
        xp = mixer_output(xp, ip, oa, ob, oc, l, lam_init, hgrn_onorm_w, diff_subln_w, w_out)
        p_bk.append(ip['b_k']); p_bv.append(ip['b_v']); p_ik.append(ip['i_k'])
        p_ck.append(ip['c_k']); p_cv.append(ip['c_v']); p_st.append(st_p)
        isp = mixer_inputs(xs, pos_s, l, norm_w, w_in, lb_all, dsa_qnorm_w, dsa_knorm_w, diff_qnorm_w, diff_knorm_w)
        oa, st_s = hgrn2_chunked(isp['a_q'], isp['a_k'], isp['a_v'], isp['a_logf'], state_hgrn[l])
        ob = dsa_sample(isp['b_q'], isp['b_k'], isp['b_v'], isp['i_q'], isp['i_w'], isp['i_k'],
                        cache_dsa_k, cache_dsa_v, cache_idx_k, page_table, pos_s, l)
        oc = diff_sample(isp['c_q'], isp['c_k'], isp['c_v'], cache_diff_k, cache_diff_v, page_table, pos_s, lam, l)
        xs = mixer_output(xs, isp, oa, ob, oc, l, lam_init, hgrn_onorm_w, diff_subln_w, w_out)
        s_bk.append(isp['b_k']); s_bv.append(isp['b_v']); s_ik.append(isp['i_k'])
        s_ck.append(isp['c_k']); s_cv.append(isp['c_v']); s_st.append(st_s)
    return (xp, xs,
            jnp.stack(p_bk), jnp.stack(p_bv), jnp.stack(p_ik), jnp.stack(p_ck), jnp.stack(p_cv), jnp.stack(p_st),
            jnp.stack(s_bk), jnp.stack(s_bv), jnp.stack(s_ik), jnp.stack(s_ck), jnp.stack(s_cv), jnp.stack(s_st))
```

```python
import functools
import math

import numpy as np
import jax
import jax.numpy as jnp
from jax import lax
from jax.experimental import pallas as pl
from jax.experimental.pallas import tpu as pltpu

F32 = jnp.float32
BF16 = jnp.bfloat16
I32 = jnp.int32

D_MODEL = 1024
A_W, A_HEADS, A_DK = 256, 4, 64
B_W, B_HEADS, B_HD = 384, 6, 64
IDX_HEADS, IDX_D = 8, 64
C_W, C_HEADS, C_DV, C_QK = 384, 4, 96, 48
TOPK_MAX = 256
PAGE_SIZE = 128
ROPE_THETA = 500000.0
EPS = 1e-6
NEG_BIG = -1e30
HGRN_CHUNK = 64
LANES = 128
INT_MIN = -(2 ** 31)
VMEM_LIMIT = 56 * 1024 * 1024
PROMPT_TILE = 256
SAMPLE_PAGE_GROUP = 16

_OFF_A = 0
_OFF_BQ, _OFF_BK, _OFF_BV, _OFF_BG = 1024, 1408, 1792, 2176
_OFF_IQ, _OFF_IK, _OFF_IW = 2560, 3072, 3200
_OFF_CQ, _OFF_CK, _OFF_CV, _OFF_CG = 3328, 3712, 4096, 4480
_N_PACK = 4864


def _dot(a, b):
    return jnp.dot(a, b, preferred_element_type=F32)


def _dot_nt(a, b):
    return lax.dot_general(a, b, (((1,), (1,)), ((), ())), preferred_element_type=F32)


def _pack_w_in(w):
    i_k = w[:, 3072:3136]
    i_w = w[:, 3136:3144]
    pad = jnp.zeros((w.shape[0], LANES - IDX_HEADS), w.dtype)
    packed = jnp.concatenate([w[:, :3072], i_k, i_k, i_w, pad, w[:, 3144:]], axis=1).astype(BF16)
    wv_t = jnp.concatenate([w[:, 1792:2176], w[:, 3912:4296]], axis=1).T.astype(BF16)
    return packed, wv_t


def _rope_tables(pos, d, n_rep):
    rot = d // 4
    half = rot // 2
    inv = jnp.power(ROPE_THETA, -2.0 * jnp.arange(half, dtype=F32) / rot)
    ang = pos.astype(F32)[:, None] * inv[None, :]
    cos, sin = jnp.cos(ang), jnp.sin(ang)
    t = pos.shape[0]
    one = jnp.ones((t, d - rot), F32)
    zero = jnp.zeros((t, d - half), F32)
    c = jnp.concatenate([cos, cos, one], axis=1)
    sa = jnp.concatenate([-sin, zero], axis=1)
    sb = jnp.concatenate([jnp.zeros((t, half), F32), sin, jnp.zeros((t, d - rot), F32)], axis=1)
    return jnp.stack([jnp.tile(c, (1, n_rep)), jnp.tile(sa, (1, n_rep)), jnp.tile(sb, (1, n_rep))])


def _block_diag_ones(width, group):
    g = np.arange(width) // group
    return jnp.asarray((g[:, None] == g[None, :]).astype(np.float32), dtype=BF16)


def _pad_perm():
    p = np.zeros((C_W, C_HEADS * LANES), np.float32)
    for j in range(C_W):
        p[j, (j // C_DV) * LANES + (j % C_DV)] = 1.0
    return jnp.asarray(p, dtype=BF16)


def _rope(x, tab_ref, width, half):
    c = tab_ref[0, :, :width]
    sa = tab_ref[1, :, :width]
    sb = tab_ref[2, :, :width]
    return x * c + pltpu.roll(x, width - half, 1) * sa + pltpu.roll(x, half, 1) * sb


def _head_rms(x, m_ref, group, w_row):
    ms = _dot((x * x).astype(BF16), m_ref[...]) * (1.0 / group)
    return x * lax.rsqrt(ms + EPS) * w_row


def _inproj_kernel(layer, x_ref, nw_ref, w_ref, wvt_ref, lbl_ref, t64_ref, t48_ref, m64_ref, m48_ref, ppad_ref,
                   bqn_ref, bkn_ref, cqn_ref, ckn_ref,
                   hg_ref, gate_ref, bkt_ref, bvt32_ref, ikt_ref, ckt_ref, cvt32_ref,
                   bqb_ref, bkb_ref, bvt_ref, iqb_ref, ikd_ref, iwt_ref, cqb_ref, cqp_ref, ckp_ref, cvt_ref):
    xf = x_ref[0]
    y = xf * lax.rsqrt(jnp.mean(xf * xf, axis=-1, keepdims=True) + EPS)
    h = (y * nw_ref[...]).astype(BF16)

    def seg(a, b):
        return _dot(h, w_ref[:, a:b])

    lg = lbl_ref[...]
    e = jnp.exp(lg - jnp.max(lg, axis=0, keepdims=True))
    soft = e / jnp.sum(e, axis=0, keepdims=True)
    lb = jnp.zeros((1, A_W), F32)
    for i in range(1, layer + 1):
        lb = lb + soft[i:i + 1, :]
    fz = seg(_OFF_A + 256, _OFF_A + 512)
    en = jnp.exp(-jnp.abs(fz))
    r = 1.0 / (1.0 + en)
    pos_side = fz >= 0
    sig_p = jnp.where(pos_side, r, en * r)
    sig_n = jnp.where(pos_side, en * r, r)
    hg_ref[0, :, 0:256] = seg(_OFF_A, _OFF_A + 256)
    hg_ref[0, :, 256:512] = (1.0 - lb) * sig_n
    hg_ref[0, :, 512:768] = jnp.log(lb + (1.0 - lb) * sig_p)
    hg_ref[0, :, 768:1024] = seg(_OFF_A + 512, _OFF_A + 768)

    def silu(g):
        return g / (1.0 + jnp.exp(-g))

    gate_ref[0, :, 0:256] = silu(seg(_OFF_A + 768, _OFF_A + 1024))
    gate_ref[0, :, 256:640] = silu(seg(_OFF_BG, _OFF_BG + B_W))
    gate_ref[0, :, 640:1024] = silu(seg(_OFF_CG, _OFF_CG + C_W))

    bq = _rope(_head_rms(seg(_OFF_BQ, _OFF_BQ + B_W), m64_ref, B_HD, bqn_ref[...]), t64_ref, B_W, B_HD // 8)
    bqb_ref[0] = (bq * (B_HD ** -0.5)).astype(BF16)
    bk = _rope(_head_rms(seg(_OFF_BK, _OFF_BK + B_W), m64_ref, B_HD, bkn_ref[...]), t64_ref, B_W, B_HD // 8)
    bkt_ref[0] = bk.T
    bkb_ref[0] = bk.astype(BF16)
    bvt = _dot_nt(wvt_ref[0:B_W, :], h)
    bvt32_ref[0] = bvt
    bvt_ref[0, 0] = bvt.astype(BF16)

    iq = _rope(seg(_OFF_IQ, _OFF_IQ + IDX_HEADS * IDX_D), t64_ref, IDX_HEADS * IDX_D, IDX_D // 8)
    iqb_ref[0] = (iq * (IDX_D ** -0.5)).astype(BF16)
    ikd = _rope(seg(_OFF_IK, _OFF_IK + LANES), t64_ref, LANES, IDX_D // 8)
    ikt_ref[0] = ikd.T[0:IDX_D, :]
    ikd_ref[0] = ikd.astype(BF16)
    iw = seg(_OFF_IW, _OFF_IW + LANES) * (IDX_HEADS ** -0.5)
    iwt_ref[0] = iw.T[0:IDX_HEADS, :]

    cq = _rope(_head_rms(seg(_OFF_CQ, _OFF_CQ + C_W), m48_ref, C_QK, cqn_ref[...]), t48_ref, C_W, C_QK // 8)
    cqb = (cq * (C_QK ** -0.5)).astype(BF16)
    cqb_ref[0] = cqb
    cqp_ref[0] = _dot(cqb, ppad_ref[...]).astype(BF16)
    ck = _rope(_head_rms(seg(_OFF_CK, _OFF_CK + C_W), m48_ref, C_QK, ckn_ref[...]), t48_ref, C_W, C_QK // 8)
    ckt_ref[0] = ck.T
    ckp_ref[0] = _dot(ck.astype(BF16), ppad_ref[...]).astype(BF16)
    cvt = _dot_nt(wvt_ref[B_W:B_W + C_W, :], h)
    cvt32_ref[0] = cvt
    cvt_ref[0, 0] = cvt.astype(BF16)


def _inproj(layer, x, pos, norm_w, w_packs, lb_logits, bqn, bkn, cqn, ckn, tm):
    w_pack, wv_t = w_packs
    bn, t, d = x.shape
    nt = t // tm
    t64 = _rope_tables(pos, IDX_D, IDX_HEADS)
    t48 = _rope_tables(pos, C_QK, 2 * C_HEADS)
    m64 = _block_diag_ones(B_W, B_HD)
    m48 = _block_diag_ones(C_W, C_QK)
    ppad = _pad_perm()

    def tok(width, dtype):
        return jax.ShapeDtypeStruct((bn, t, width), dtype), pl.BlockSpec((1, tm, width), lambda i, b: (b, i, 0))

    def trf(rows):
        return jax.ShapeDtypeStruct((bn, rows, t), F32), pl.BlockSpec((1, rows, tm), lambda i, b: (b, 0, i))

    def tr(rows, dtype):
        return (jax.ShapeDtypeStruct((bn, nt, rows, tm), dtype),
                pl.BlockSpec((1, 1, rows, tm), lambda i, b: (b, i, 0, 0)))

    outs = dict(
        hg=tok(1024, F32), gate=tok(1024, F32), bkt=trf(B_W), bvt32=trf(B_W), ikt=trf(IDX_D), ckt=trf(C_W), cvt32=trf(C_W),
        bqb=tok(B_W, BF16), bkb=tok(B_W, BF16), bvt=tr(B_W, BF16), iqb=tok(IDX_HEADS * IDX_D, BF16),
        ikd=tok(LANES, BF16),
        iwt=(jax.ShapeDtypeStruct((bn, IDX_HEADS, t), F32), pl.BlockSpec((1, IDX_HEADS, tm), lambda i, b: (b, 0, i))),
        cqb=tok(C_W, BF16), cqp=tok(C_HEADS * LANES, BF16), ckp=tok(C_HEADS * LANES, BF16), cvt=tr(C_W, BF16),
    )
    names = list(outs)

    def const(shape):
        nd = len(shape)
        return pl.BlockSpec(shape, lambda i, b: (0,) * nd)

    in_specs = [
        pl.BlockSpec((1, tm, d), lambda i, b: (b, i, 0)),
        const((1, d)),
        const((d, _N_PACK)),
        const((B_W + C_W, d)),
        const(lb_logits.shape),
        pl.BlockSpec((3, tm, IDX_HEADS * IDX_D), lambda i, b: (0, i, 0)),
        pl.BlockSpec((3, tm, C_W), lambda i, b: (0, i, 0)),
        const((B_W, B_W)), const((C_W, C_W)), const((C_W, C_HEADS * LANES)),
        const((1, B_W)), const((1, B_W)), const((1, C_W)), const((1, C_W)),
    ]
    res = pl.pallas_call(
        functools.partial(_inproj_kernel, layer),
        grid=(nt, bn),
        in_specs=in_specs,
        out_specs=[outs[n][1] for n in names],
        out_shape=[outs[n][0] for n in names],
        compiler_params=pltpu.CompilerParams(dimension_semantics=("arbitrary", "arbitrary"),
                                             vmem_limit_bytes=VMEM_LIMIT),
        name=f"inproj_l{layer}_t{t}",
    )(x, norm_w.reshape(1, d), w_pack, wv_t, lb_logits, t64, t48, m64, m48, ppad,
      jnp.tile(bqn, B_HEADS).reshape(1, B_W), jnp.tile(bkn, B_HEADS).reshape(1, B_W),
      jnp.tile(cqn, 2 * C_HEADS).reshape(1, C_W), jnp.tile(ckn, 2 * C_HEADS).reshape(1, C_W))
    return dict(zip(names, res))


_HGRN_LEVELS = (1, 2, 4, 8, 16, 32)


def _hgrn_tables():
    c = HGRN_CHUNK
    t = np.arange(c)[:, None]
    u = np.arange(c)[None, :]
    mats = [(u <= t), (u > t)]
    masks = [(t == u)]
    for h in _HGRN_LEVELS:
        r = (t // (2 * h)) * (2 * h) + h - 1
        right = (t % (2 * h)) >= h
        mats.append(np.where(right, (u > r) & (u <= t), (u > t) & (u <= r)))
        masks.append((t // (2 * h) == u // (2 * h)) & right & ((u % (2 * h)) < h))
    tall = np.concatenate(mats, axis=0).astype(np.float32)
    mk = np.stack([np.tile(m, (1, 2)) for m in masks]).astype(np.float32)
    return jnp.asarray(tall, dtype=BF16), jnp.asarray(mk)


def _pair_expand(a):
    lane = lax.broadcasted_iota(I32, a.shape, 1)
    zero = jnp.zeros_like(a)
    return jnp.concatenate([jnp.where(lane < A_DK, a, zero), jnp.where(lane >= A_DK, a, zero)], axis=0)


def _hgrn_kernel(nchunk, hg_ref, tall_ref, mk_ref, s0_ref, o_ref, sf_ref, st_ref):
    c = HGRN_CHUNK

    @pl.when(pl.program_id(1) == 0)
    def _():
        st_ref[...] = s0_ref[0]

    row = lax.broadcasted_iota(I32, (2 * A_DK, 2 * A_DK), 0)
    col = lax.broadcasted_iota(I32, (2 * A_DK, 2 * A_DK), 1)
    same_head = (row // A_DK) == (col // A_DK)
    zpad = jnp.zeros((c, 2 * A_DK), BF16)

    for ci in range(nchunk):
        rows = slice(ci * c, (ci + 1) * c)
        lf = hg_ref[0, rows, 512:768]
        l1 = lf.astype(BF16)
        r1 = lf - l1.astype(F32)
        l2 = r1.astype(BF16)
        l3 = (r1 - l2.astype(F32)).astype(BF16)
        e3 = _dot(tall_ref[...], jnp.concatenate([l1, l2, l3], axis=1))
        ex = jnp.exp(e3[:, 0:A_W] + e3[:, A_W:2 * A_W] + e3[:, 2 * A_W:3 * A_W])
        x_cum = ex[0:c]
        x_rem = ex[c:2 * c]
        for p in range(2):
            lanes = slice(p * 128, (p + 1) * 128)
            qp = hg_ref[0, rows, lanes]
            kp = hg_ref[0, rows, 256 + p * 128:256 + (p + 1) * 128]
            vp = hg_ref[0, rows, 768 + p * 128:768 + (p + 1) * 128].astype(BF16)
            st = st_ref[p]
            o = _dot_nt((qp * x_cum[:, lanes]).astype(BF16), st.astype(BF16))
            sc = mk_ref[0] * _dot_nt(qp.astype(BF16), _pair_expand(kp.astype(BF16)))
            for li in range(len(_HGRN_LEVELS)):
                xl = ex[(2 + li) * c:(3 + li) * c, lanes]
                sc = sc + mk_ref[1 + li] * _dot_nt((qp * xl).astype(BF16), _pair_expand((kp * xl).astype(BF16)))
            o = o + _dot(sc.astype(BF16), _pair_expand(vp))
            o_ref[0, rows, lanes] = o
            kh = (kp * x_rem[:, lanes]).astype(BF16)
            vt = jnp.concatenate([vp, zpad], axis=0).astype(F32).T.astype(BF16)
            upd = _dot(vt, jnp.concatenate([kh, zpad], axis=0))
            st_ref[p] = jnp.where(same_head, st * x_cum[c - 1:c, lanes] + upd, 0.0)
    sf_ref[0] = st_ref[...]


def _state_to_pairs(s):
    bn = s.shape[0]
    st = jnp.swapaxes(s, -1, -2).reshape(bn, 2, 2, A_DK, A_DK)
    z = jnp.zeros((bn, 2, A_DK, A_DK), s.dtype)
    top = jnp.concatenate([st[:, :, 0], z], axis=-1)
    bot = jnp.concatenate([z, st[:, :, 1]], axis=-1)
    return jnp.concatenate([top, bot], axis=-2)


def _pairs_to_state(sp):
    bn = sp.shape[0]
    h0 = sp[:, :, :A_DK, :A_DK]
    h1 = sp[:, :, A_DK:, A_DK:]
    return jnp.swapaxes(jnp.stack([h0, h1], axis=2).reshape(bn, A_HEADS, A_DK, A_DK), -1, -2)


def _hgrn(hg, s0, tb, name):
    bn, t, _ = hg.shape
    tall, mk = _hgrn_tables()
    o, sf = pl.pallas_call(
        functools.partial(_hgrn_kernel, tb // HGRN_CHUNK),
        grid=(bn, t // tb),
        in_specs=[pl.BlockSpec((1, tb, 1024), lambda b, j: (b, j, 0)),
                  pl.BlockSpec(tall.shape, lambda b, j: (0, 0)),
                  pl.BlockSpec(mk.shape, lambda b, j: (0, 0, 0)),
                  pl.BlockSpec((1, 2, 128, 128), lambda b, j: (b, 0, 0, 0))],
        out_specs=[pl.BlockSpec((1, tb, A_W), lambda b, j: (b, j, 0)),
                   pl.BlockSpec((1, 2, 128, 128), lambda b, j: (b, 0, 0, 0))],
        out_shape=[jax.ShapeDtypeStruct((bn, t, A_W), F32), jax.ShapeDtypeStruct((bn, 2, 128, 128), F32)],
        scratch_shapes=[pltpu.VMEM((2, 128, 128), F32)],
        compiler_params=pltpu.CompilerParams(dimension_semantics=("arbitrary", "arbitrary"),
                                             vmem_limit_bytes=VMEM_LIMIT),
        name=name,
    )(hg, tall, mk, _state_to_pairs(s0))
    return o, _pairs_to_state(sf)


DSA_TQ = 128


def _key_of(score):
    bits = lax.bitcast_convert_type(score, I32)
    key = bits ^ ((bits >> 31) & 0x7FFFFFFF)
    return jnp.where(score == 0.0, 0, key)


def _count_rows(pred_fn, nk, ch, width):
    def body(kc, acc):
        hit = jnp.where(pred_fn(kc), 1, 0)
        return acc + jnp.sum(hit.reshape(ch // 8, 8, width), axis=0)

    acc = lax.fori_loop(0, nk, body, jnp.zeros((8, width), I32))
    return jnp.sum(acc, axis=0, keepdims=True)


def _select_topk(keys_ref, nk, ch, width, n_keep, pos_bits):
    def chunk(kc):
        return keys_ref[pl.ds(kc * ch, ch), :]

    def kpos(kc):
        return kc * ch + lax.broadcasted_iota(I32, (ch, width), 0)

    def bit_step(i, tu):
        cand = tu | jnp.left_shift(jnp.int32(1), 31 - i)
        cnt = _count_rows(lambda kc: chunk(kc) >= (cand ^ INT_MIN), nk, ch, width)
        return jnp.where(cnt >= n_keep, cand, tu)

    thr = lax.fori_loop(0, 32, bit_step, jnp.zeros((1, width), I32)) ^ INT_MIN
    cnt_ge = _count_rows(lambda kc: chunk(kc) >= thr, nk, ch, width)
    excess = jnp.where(thr > INT_MIN, cnt_ge - n_keep, 0)

    @pl.when(jnp.max(excess) > 0)
    def _():
        need = n_keep - _count_rows(lambda kc: chunk(kc) > thr, nk, ch, width)

        def pos_step(i, c):
            cand = c | jnp.left_shift(jnp.int32(1), pos_bits - 1 - i)
            before = _count_rows(lambda kc: (chunk(kc) == thr) & (kpos(kc) < cand), nk, ch, width)
            return jnp.where(before <= need - 1, cand, c)

        cut = lax.fori_loop(0, pos_bits, pos_step, jnp.zeros((1, width), I32))

        def drop(kc, carry):
            k = chunk(kc)
            late_tie = (k == thr) & (kpos(kc) > cut) & (excess > 0)
            keys_ref[pl.ds(kc * ch, ch), :] = jnp.where(late_tie, INT_MIN, k)
            return carry

        lax.fori_loop(0, nk, drop, 0)

    return jnp.maximum(thr, INT_MIN + 1)


def _lane_halves(x):
    lane = lax.broadcasted_iota(I32, x.shape, 1)
    zero = jnp.zeros_like(x)
    return jnp.concatenate([jnp.where(lane < B_HD, x, zero), jnp.where(lane >= B_HD, x, zero)], axis=0)


def _dsa_prompt_kernel(n_keep, ch, pos_bits, iqb_ref, iwt_ref, ikd_ref, bqb_ref, bkb_ref, bvt_ref, o_ref, keys_ref):
    tq = DSA_TQ
    qb = pl.program_id(1)
    nk = (qb * tq + tq + ch - 1) // ch
    qpos = qb * tq + lax.broadcasted_iota(I32, (1, tq), 1)

    iq = iqb_ref[0]
    iq_rows = jnp.concatenate([_lane_halves(iq[:, j * LANES:(j + 1) * LANES]) for j in range(IDX_HEADS // 2)], axis=0)
    wt = iwt_ref[0]

    def index_chunk(kc, carry):
        s = _dot_nt(ikd_ref[0, pl.ds(kc * ch, ch), :], iq_rows)
        acc = jnp.zeros((ch, tq), F32)
        for h in range(IDX_HEADS):
            acc = acc + wt[h:h + 1, :] * jnp.maximum(s[:, h * tq:(h + 1) * tq], 0.0)
        kp = kc * ch + lax.broadcasted_iota(I32, (ch, tq), 0)
        keys_ref[pl.ds(kc * ch, ch), :] = jnp.where(kp <= qpos, _key_of(acc), INT_MIN)
        return carry

    lax.fori_loop(0, nk, index_chunk, 0)
    thr = _select_topk(keys_ref, nk, ch, tq, n_keep, pos_bits)

    bq = bqb_ref[0]
    outs = []
    for p in range(B_HEADS // 2):
        lanes = slice(p * LANES, (p + 1) * LANES)
        q_rows = _lane_halves(bq[:, lanes])

        def attend(kc, carry, lanes=lanes, q_rows=q_rows):
            m, l, acc = carry
            s = _dot_nt(bkb_ref[0, pl.ds(kc * ch, ch), lanes], q_rows)
            sel = keys_ref[pl.ds(kc * ch, ch), :] >= thr
            s = jnp.concatenate([jnp.where(sel, s[:, :tq], NEG_BIG), jnp.where(sel, s[:, tq:], NEG_BIG)], axis=1)
            m_new = jnp.maximum(m, jnp.max(s, axis=0, keepdims=True))
            alpha = jnp.exp(m - m_new)
            e = jnp.exp(s - m_new)
            pr = jnp.concatenate([jnp.where(sel, e[:, :tq], 0.0), jnp.where(sel, e[:, tq:], 0.0)], axis=1)
            l = alpha * l + jnp.sum(pr, axis=0, keepdims=True)
            acc = alpha * acc + _dot(bvt_ref[0, kc, lanes, :], pr.astype(BF16))
            return m_new, l, acc

        init = (jnp.full((1, 2 * tq), NEG_BIG, F32), jnp.zeros((1, 2 * tq), F32), jnp.zeros((LANES, 2 * tq), F32))
        m, l, acc = lax.fori_loop(0, nk, attend, init)
        inv = 1.0 / l
        outs.append(acc[0:B_HD, 0:tq] * inv[:, 0:tq])
        outs.append(acc[B_HD:, tq:] * inv[:, tq:])
    o_ref[0] = jnp.concatenate(outs, axis=0).T


def _dsa_prompt(o, ch):
    bn, s, _ = o['bkb'].shape
    n_keep = min(TOPK_MAX, s // 4)
    nb = s // DSA_TQ
    return pl.pallas_call(
        functools.partial(_dsa_prompt_kernel, n_keep, ch, s.bit_length()),
        grid=(bn, nb),
        in_specs=[pl.BlockSpec((1, DSA_TQ, IDX_HEADS * IDX_D), lambda b, q: (b, q, 0)),
                  pl.BlockSpec((1, IDX_HEADS, DSA_TQ), lambda b, q: (b, 0, q)),
                  pl.BlockSpec((1, s, LANES), lambda b, q: (b, 0, 0)),
                  pl.BlockSpec((1, DSA_TQ, B_W), lambda b, q: (b, q, 0)),
                  pl.BlockSpec((1, s, B_W), lambda b, q: (b, 0, 0)),
                  pl.BlockSpec((1, s // ch, B_W, ch), lambda b, q: (b, 0, 0, 0))],
        out_specs=pl.BlockSpec((1, DSA_TQ, B_W), lambda b, q: (b, q, 0)),
        out_shape=jax.ShapeDtypeStruct((bn, s, B_W), F32),
        scratch_shapes=[pltpu.VMEM((s, DSA_TQ), I32)],
        compiler_params=pltpu.CompilerParams(dimension_semantics=("arbitrary", "arbitrary"),
                                             vmem_limit_bytes=VMEM_LIMIT),
        name="dsa_prompt",
    )(o['iqb'], o['iwt'], o['ikd'], o['bqb'], o['bkb'], o['bvt'])


def _diff_lambda(dl, lam_init):
    a = jnp.sum(dl[0:1, :] * dl[1:2, :], axis=1, keepdims=True)
    b = jnp.sum(dl[2:3, :] * dl[3:4, :], axis=1, keepdims=True)
    return jnp.exp(a) - jnp.exp(b) + lam_init


def _diff_prompt_kernel(lam_init, ch, dl_ref, cqp_ref, ckp_ref, cvt_ref, o_ref):
    tq = DSA_TQ
    qb = pl.program_id(1)
    nk = (qb * tq + tq + ch - 1) // ch
    qpos = qb * tq + lax.broadcasted_iota(I32, (1, tq), 1)
    lam = _diff_lambda(dl_ref[...], lam_init)
    cq = cqp_ref[0]
    lane = lax.broadcasted_iota(I32, (tq, LANES), 1)
    outs = []
    for h in range(C_HEADS):
        lanes = slice(h * LANES, (h + 1) * LANES)
        qh = cq[:, lanes]
        zero = jnp.zeros_like(qh)
        q_rows = jnp.concatenate([jnp.where(lane < C_QK, qh, zero),
                                  jnp.where((lane >= C_QK) & (lane < 2 * C_QK), qh, zero)], axis=0)

        def attend(kc, carry, h=h, lanes=lanes, q_rows=q_rows):
            m, l, acc = carry
            s = _dot_nt(ckp_ref[0, pl.ds(kc * ch, ch), lanes], q_rows)
            vis = (kc * ch + lax.broadcasted_iota(I32, (ch, tq), 0)) <= qpos
            s = jnp.concatenate([jnp.where(vis, s[:, :tq], NEG_BIG), jnp.where(vis, s[:, tq:], NEG_BIG)], axis=1)
            m_new = jnp.maximum(m, jnp.max(s, axis=0, keepdims=True))
            alpha = jnp.exp(m - m_new)
            e = jnp.exp(s - m_new)
            pr = jnp.concatenate([jnp.where(vis, e[:, :tq], 0.0), jnp.where(vis, e[:, tq:], 0.0)], axis=1)
            l = alpha * l + jnp.sum(pr, axis=0, keepdims=True)
            acc = alpha * acc + _dot(cvt_ref[0, kc, h * C_DV:(h + 1) * C_DV, :], pr.astype(BF16))
            return m_new, l, acc

        init = (jnp.full((1, 2 * tq), NEG_BIG, F32), jnp.zeros((1, 2 * tq), F32), jnp.zeros((C_DV, 2 * tq), F32))
        m, l, acc = lax.fori_loop(0, nk, attend, init)
        inv = 1.0 / l
        outs.append(acc[:, :tq] * inv[:, :tq] - lam * (acc[:, tq:] * inv[:, tq:]))
    o_ref[0] = jnp.concatenate(outs, axis=0).T


def _diff_prompt(layer, o, dl, ch):
    bn, s, _ = o['ckp'].shape
    lam_init = 0.8 - 0.6 * math.exp(-0.3 * layer)
    return pl.pallas_call(
        functools.partial(_diff_prompt_kernel, lam_init, ch),
        grid=(bn, s // DSA_TQ),
        in_specs=[pl.BlockSpec((4, C_QK), lambda b, q: (0, 0)),
                  pl.BlockSpec((1, DSA_TQ, C_HEADS * LANES), lambda b, q: (b, q, 0)),
                  pl.BlockSpec((1, s, C_HEADS * LANES), lambda b, q: (b, 0, 0)),
                  pl.BlockSpec((1, s // ch, C_W, ch), lambda b, q: (b, 0, 0, 0))],
        out_specs=pl.BlockSpec((1, DSA_TQ, C_W), lambda b, q: (b, q, 0)),
        out_shape=jax.ShapeDtypeStruct((bn, s, C_W), F32),
        compiler_params=pltpu.CompilerParams(dimension_semantics=("arbitrary", "arbitrary"),
                                             vmem_limit_bytes=VMEM_LIMIT),
        name="diff_prompt",
    )(dl, o['cqp'], o['ckp'], o['cvt'])


def _token_minor(cache):
    nd = cache.ndim
    c = jnp.transpose(cache, (0, 1) + tuple(range(3, nd)) + (2,))
    return c.reshape(c.shape[0], c.shape[1], -1, c.shape[-1])


def _page_specs(layer, n, group, width):
    def spec(j):
        return pl.BlockSpec((None, None, width, PAGE_SIZE), lambda b, g, pt: (layer, pt[b, g * group + j], 0, 0))

    return [spec(j) for _ in range(n) for j in range(group)]


def _rows_of_queries(k4, t_new):
    return jnp.concatenate([jnp.broadcast_to(k4[t:t + 1, :], (8, k4.shape[1])) for t in range(t_new)], axis=0)


def _select_topk_rows(keys_ref, lp, n_keep, pos_bits):
    cw = 16 * LANES

    def count(pred):
        acc = jnp.zeros((8, LANES), I32)
        for c0 in range(0, lp, cw):
            w = min(cw, lp - c0)
            hit = jnp.where(pred(keys_ref[0, :, c0:c0 + w], c0, w), 1, 0)
            for j in range(w // LANES):
                acc = acc + hit[:, j * LANES:(j + 1) * LANES]
        return jnp.sum(acc.astype(F32), axis=1, keepdims=True).astype(I32)

    def kpos(c0, w):
        return c0 + lax.broadcasted_iota(I32, (8, w), 1)

    def bit_step(i, tu):
        cand = tu | jnp.left_shift(jnp.int32(1), 31 - i)
        cnt = count(lambda k, c0, w: k >= (cand ^ INT_MIN))
        return jnp.where(cnt >= n_keep, cand, tu)

    thr = lax.fori_loop(0, 32, bit_step, jnp.zeros((8, 1), I32)) ^ INT_MIN
    excess = jnp.where(thr > INT_MIN, count(lambda k, c0, w: k >= thr) - n_keep, 0)

    @pl.when(jnp.max(excess) > 0)
    def _():
        need = n_keep - count(lambda k, c0, w: k > thr)

        def pos_step(i, c):
            cand = c | jnp.left_shift(jnp.int32(1), pos_bits - 1 - i)
            before = count(lambda k, c0, w: (k == thr) & (kpos(c0, w) < cand))
            return jnp.where(before <= need - 1, cand, c)

        cut = lax.fori_loop(0, pos_bits, pos_step, jnp.zeros((8, 1), I32))
        for c0 in range(0, lp, cw):
            w = min(cw, lp - c0)
            k = keys_ref[0, :, c0:c0 + w]
            late_tie = (k == thr) & (kpos(c0, w) > cut) & (excess > 0)
            keys_ref[0, :, c0:c0 + w] = jnp.where(late_tie, INT_MIN, k)

    return jnp.maximum(thr, INT_MIN + 1)


def _sample_index_kernel(group, n_groups, t_new, n_keep, pos_bits, pt_ref, *refs):
    pages = refs[:group]
    iq_ref, w_ref, iknew_ref, keys_ref, thr_ref = refs[group:]
    g = pl.program_id(1)
    gk = group * PAGE_SIZE
    past = n_groups * gk
    iq = iq_ref[0]
    wcol = w_ref[0][:, 0:1]

    def scores(kmat_t):
        s = jnp.maximum(_dot(iq, kmat_t), 0.0) * wcol
        return jnp.concatenate([jnp.sum(s[8 * t:8 * t + 8, :], axis=0, keepdims=True) for t in range(t_new)], axis=0)

    kcat = jnp.concatenate([p[...].astype(BF16) for p in pages], axis=1)
    key = _key_of(scores(kcat))
    key8 = jnp.concatenate([key] * (8 // t_new), axis=0)
    for gg in range(n_groups):
        @pl.when(g == gg)
        def _(gg=gg):
            keys_ref[0, :, gg * gk:(gg + 1) * gk] = key8

    @pl.when(g == n_groups - 1)
    def _():
        kn = _key_of(scores(iknew_ref[0]))
        row = lax.broadcasted_iota(I32, kn.shape, 0)
        lane = lax.broadcasted_iota(I32, kn.shape, 1)
        kn = jnp.where((lane <= row) & (lane < t_new), kn, INT_MIN)
        keys_ref[0, :, past:past + LANES] = jnp.concatenate([kn] * (8 // t_new), axis=0)
        thr = _select_topk_rows(keys_ref, past + LANES, n_keep, pos_bits)
        thr_ref[0] = jnp.broadcast_to(thr, (8, LANES))


def _sample_index(layer, cache_idx_k, page_table, iq_rows, w_rows, iknew, group, n_keep):
    bd, n_pages = page_table.shape
    n_groups = n_pages // group
    t_new = iq_rows.shape[1] // 8
    past = n_pages * PAGE_SIZE
    lp = past + LANES
    grid_spec = pltpu.PrefetchScalarGridSpec(
        num_scalar_prefetch=1,
        grid=(bd, n_groups),
        in_specs=_page_specs(layer, 1, group, IDX_D) + [
            pl.BlockSpec((1, 8 * t_new, IDX_D), lambda b, g, pt: (b, 0, 0)),
            pl.BlockSpec((1, 8 * t_new, LANES), lambda b, g, pt: (b, 0, 0)),
            pl.BlockSpec((1, IDX_D, PAGE_SIZE), lambda b, g, pt: (b, 0, 0))],
        out_specs=[pl.BlockSpec((1, 8, lp), lambda b, g, pt: (b, 0, 0)),
                   pl.BlockSpec((1, 8, LANES), lambda b, g, pt: (b, 0, 0))],
    )
    return pl.pallas_call(
        functools.partial(_sample_index_kernel, group, n_groups, t_new, n_keep, lp.bit_length()),
        grid_spec=grid_spec,
        out_shape=[jax.ShapeDtypeStruct((bd, 8, lp), I32), jax.ShapeDtypeStruct((bd, 8, LANES), I32)],
        compiler_params=pltpu.CompilerParams(dimension_semantics=("arbitrary", "arbitrary"),
                                             vmem_limit_bytes=VMEM_LIMIT),
        name=f"sample_index_l{layer}",
    )(page_table, *([cache_idx_k] * group), iq_rows, w_rows, iknew)


def _softmax_step(s, valid, v_t, m_ref, l_ref, acc_ref):
    if valid is not None:
        s = jnp.where(valid, s, NEG_BIG)
    m_old = m_ref[:, 0:1]
    m_new = jnp.maximum(m_old, jnp.max(s, axis=1, keepdims=True))
    alpha = jnp.exp(m_old - m_new)
    e = jnp.exp(s - m_new)
    if valid is not None:
        e = jnp.where(valid, e, 0.0)
    l_new = alpha * l_ref[:, 0:1] + jnp.sum(e, axis=1, keepdims=True)
    acc_ref[...] = alpha * acc_ref[...] + _dot_nt(e.astype(BF16), v_t)
    m_ref[...] = jnp.broadcast_to(m_new, m_ref.shape)
    l_ref[...] = jnp.broadcast_to(l_new, l_ref.shape)


def _sample_attn_kernel(group, n_groups, t_new, lam_init, pt_ref, *refs):
    n = group
    dk_pages, dv_pages, ck_pages, cv_pages = refs[0:n], refs[n:2 * n], refs[2 * n:3 * n], refs[3 * n:4 * n]
    (keys_ref, keysn_ref, thr_ref, qd_ref, qc_ref, dkn_ref, dvn_ref, ckn_ref, cvn_ref, dl_ref,
     ob_ref, oc_ref, md_ref, ld_ref, ad_ref, mc_ref, lc_ref, ac_ref) = refs[4 * n:]
    g = pl.program_id(1)
    rows = 8 * t_new

    @pl.when(g == 0)
    def _():
        for m_ref, l_ref, a_ref in ((md_ref, ld_ref, ad_ref), (mc_ref, lc_ref, ac_ref)):
            m_ref[...] = jnp.full(m_ref.shape, NEG_BIG, F32)
            l_ref[...] = jnp.zeros(l_ref.shape, F32)
            a_ref[...] = jnp.zeros(a_ref.shape, F32)

    def cat(pages):
        return jnp.concatenate([p[...].astype(BF16) for p in pages], axis=1)

    thr = _rows_of_queries(thr_ref[0], t_new)[:, 0:1]
    qd = qd_ref[0]
    qc = qc_ref[0]
    sel = _rows_of_queries(keys_ref[0], t_new) >= thr
    _softmax_step(_dot(qd, cat(dk_pages)), sel, cat(dv_pages), md_ref, ld_ref, ad_ref)
    _softmax_step(_dot(qc, cat(ck_pages)), None, cat(cv_pages), mc_ref, lc_ref, ac_ref)

    @pl.when(g == n_groups - 1)
    def _():
        row = lax.broadcasted_iota(I32, (rows, LANES), 0)
        lane = lax.broadcasted_iota(I32, (rows, LANES), 1)
        causal = (lane <= row // 8) & (lane < t_new)
        seln = causal & (_rows_of_queries(keysn_ref[0], t_new) >= thr)
        _softmax_step(_dot(qd, dkn_ref[0]), seln, dvn_ref[0], md_ref, ld_ref, ad_ref)
        _softmax_step(_dot(qc, ckn_ref[0]), causal, cvn_ref[0], mc_ref, lc_ref, ac_ref)

        r = lax.broadcasted_iota(I32, (rows, B_W), 0) % 8
        ln = lax.broadcasted_iota(I32, (rows, B_W), 1)
        od = jnp.where(ln // B_HD == r, ad_ref[...] / ld_ref[:, 0:1], 0.0)
        lam = _diff_lambda(dl_ref[...], lam_init)
        coef = jnp.where(r % 2 == 0, 1.0, -lam)
        oc = jnp.where(ln // C_DV == r // 2, coef * (ac_ref[...] / lc_ref[:, 0:1]), 0.0)
        pad = jnp.zeros((8 - t_new, B_W), F32)
        ob_ref[0] = jnp.concatenate([jnp.sum(od[8 * t:8 * t + 8], axis=0, keepdims=True) for t in range(t_new)] + [pad],
                                    axis=0)
        oc_ref[0] = jnp.concatenate([jnp.sum(oc[8 * t:8 * t + 8], axis=0, keepdims=True) for t in range(t_new)] + [pad],
                                    axis=0)


def _sample_attn(layer, caches, page_table, keys, thr, qd, qc, new_kv, dl, group):
    bd, n_pages = page_table.shape
    n_groups = n_pages // group
    rows = qd.shape[1]
    t_new = rows // 8
    gk = group * PAGE_SIZE
    past = n_pages * PAGE_SIZE
    lam_init = 0.8 - 0.6 * math.exp(-0.3 * layer)

    def per_seq(shape):
        nd = len(shape)
        return pl.BlockSpec((1,) + shape, lambda b, g, pt: (b,) + (0,) * nd)

    grid_spec = pltpu.PrefetchScalarGridSpec(
        num_scalar_prefetch=1,
        grid=(bd, n_groups),
        in_specs=_page_specs(layer, 4, group, B_W) + [
            pl.BlockSpec((1, 8, gk), lambda b, g, pt: (b, 0, g)),
            pl.BlockSpec((1, 8, LANES), lambda b, g, pt: (b, 0, past // LANES)),
            per_seq((8, LANES)), per_seq((rows, B_W)), per_seq((rows, C_W)),
            per_seq((B_W, PAGE_SIZE)), per_seq((B_W, PAGE_SIZE)), per_seq((C_W, PAGE_SIZE)), per_seq((C_W, PAGE_SIZE)),
            pl.BlockSpec((4, C_QK), lambda b, g, pt: (0, 0))],
        out_specs=[per_seq((8, B_W)), per_seq((8, C_W))],
        scratch_shapes=[pltpu.VMEM((rows, LANES), F32), pltpu.VMEM((rows, LANES), F32), pltpu.VMEM((rows, B_W), F32),
                        pltpu.VMEM((rows, LANES), F32), pltpu.VMEM((rows, LANES), F32), pltpu.VMEM((rows, C_W), F32)],
    )
    pages = [c for c in caches for _ in range(group)]
    return pl.pallas_call(
        functools.partial(_sample_attn_kernel, group, n_groups, t_new, lam_init),
        grid_spec=grid_spec,
        out_shape=[jax.ShapeDtypeStruct((bd, 8, B_W), F32), jax.ShapeDtypeStruct((bd, 8, C_W), F32)],
        compiler_params=pltpu.CompilerParams(dimension_semantics=("arbitrary", "arbitrary"),
                                             vmem_limit_bytes=VMEM_LIMIT),
        name=f"sample_attn_l{layer}",
    )(page_table, *pages, keys, keys, thr, qd, qc, *new_kv, dl)


def _outproj_kernel(c_scale, x_ref, oa_ref, ob_ref, oc_ref, gate_ref, w_ref, ma_ref, mc_ref, na_ref, nc_ref, y_ref):
    ga = _head_rms(oa_ref[0], ma_ref, A_DK, na_ref[...]) * gate_ref[0, :, 0:A_W]
    gb = ob_ref[0] * gate_ref[0, :, A_W:A_W + B_W]
    gc = _head_rms(oc_ref[0], mc_ref, C_DV, nc_ref[...]) * c_scale * gate_ref[0, :, A_W + B_W:]
    y = x_ref[0] + _dot(ga.astype(BF16), w_ref[0:A_W, :])
    y = y + _dot(gb.astype(BF16), w_ref[A_W:A_W + B_W, :])
    y_ref[0] = y + _dot(gc.astype(BF16), w_ref[A_W + B_W:, :])


def _outproj(layer, x, oa, ob, oc, gate, w_out, onorm_w, subln_w, tm):
    bn, t, d = x.shape
    lam_init = 0.8 - 0.6 * math.exp(-0.3 * layer)

    def tok(width):
        return pl.BlockSpec((1, tm, width), lambda b, i: (b, i, 0))

    def const(shape):
        return pl.BlockSpec(shape, lambda b, i: (0, 0))

    return pl.pallas_call(
        functools.partial(_outproj_kernel, 1.0 - lam_init),
        grid=(bn, t // tm),
        in_specs=[tok(d), tok(A_W), tok(B_W), tok(C_W), tok(d), const((d, d)), const((A_W, A_W)), const((C_W, C_W)),
                  const((1, A_W)), const((1, C_W))],
        out_specs=tok(d),
        out_shape=jax.ShapeDtypeStruct((bn, t, d), F32),
        compiler_params=pltpu.CompilerParams(dimension_semantics=("arbitrary", "arbitrary"),
                                             vmem_limit_bytes=VMEM_LIMIT),
        name=f"outproj_l{layer}_t{t}",
    )(x, oa, ob, oc, gate, w_out.astype(BF16), _block_diag_ones(A_W, A_DK), _block_diag_ones(C_W, C_DV),
      jnp.tile(onorm_w, A_HEADS).reshape(1, A_W), jnp.tile(subln_w, C_HEADS).reshape(1, C_W))


def kernel(x_prompt, x_sample, cache_dsa_k, cache_dsa_v, cache_idx_k, cache_diff_k, cache_diff_v, state_hgrn,
           page_table, norm_w, w_in, w_out, hgrn_lb_logits, hgrn_onorm_w, dsa_qnorm_w, dsa_knorm_w,
           diff_qnorm_w, diff_knorm_w, diff_lambda, diff_subln_w):
    depth = norm_w.shape[0]
    bn, s, d = x_prompt.shape
    bd, t_new, _ = x_sample.shape
    n_pool = cache_dsa_k.shape[1]
    n_pages = page_table.shape[1]
    past = n_pages * PAGE_SIZE
    group = math.gcd(SAMPLE_PAGE_GROUP, n_pages)
    pos_p = jnp.arange(s, dtype=I32)
    pos_s = past + jnp.arange(bd * t_new, dtype=I32) % t_new
    caches = [_token_minor(c) for c in (cache_dsa_k, cache_dsa_v, cache_diff_k, cache_diff_v)]
    idx_cache = _token_minor(cache_idx_k)
    xp = x_prompt
    xs = x_sample.reshape(1, bd * t_new, d)
    outs = {k: [] for k in ('p_bk', 'p_bv', 'p_ik', 'p_ck', 'p_cv', 'p_st', 's_bk', 's_bv', 's_ik', 's_ck', 's_cv', 's_st')}

    def pad_rows(a, n):
        return jnp.pad(a, ((0, 0), (0, n - a.shape[1]), (0, 0)))

    def tok_major(a_t, *feat):
        nf = len(feat)
        a = a_t.reshape((a_t.shape[0],) + feat + (a_t.shape[2],))
        return jnp.transpose(a, (0, nf + 1) + tuple(range(1, nf + 1)))

    def new_page(a_t):
        a = jnp.transpose(a_t[0].reshape(a_t.shape[1], bd, t_new), (1, 0, 2)).astype(BF16)
        return jnp.pad(a, ((0, 0), (0, 0), (0, PAGE_SIZE - t_new)))

    for l in range(depth):
        w_pack = _pack_w_in(w_in[l])
        norms = (dsa_qnorm_w[l], dsa_knorm_w[l], diff_qnorm_w[l], diff_knorm_w[l])

        o = _inproj(l, xp, pos_p, norm_w[l], w_pack, hgrn_lb_logits, *norms, tm=PROMPT_TILE)
        oa, st = _hgrn(o['hg'], jnp.zeros((bn, A_HEADS, A_DK, A_DK), F32), PROMPT_TILE, f"hgrn_prompt_l{l}")
        ob = _dsa_prompt(o, PROMPT_TILE)
        oc = _diff_prompt(l, o, diff_lambda[l], PROMPT_TILE)
        xp = _outproj(l, xp, oa, ob, oc, o['gate'], w_out[l], hgrn_onorm_w[l], diff_subln_w[l], min(512, s))
        outs['p_bk'].append(tok_major(o['bkt'], B_HEADS, B_HD))
        outs['p_bv'].append(tok_major(o['bvt32'], B_HEADS, B_HD))
        outs['p_ik'].append(tok_major(o['ikt'], IDX_D))
        outs['p_ck'].append(tok_major(o['ckt'], C_HEADS, 2, C_QK))
        outs['p_cv'].append(tok_major(o['cvt32'], C_HEADS, C_DV))
        outs['p_st'].append(st)

        o = _inproj(l, xs, pos_s, norm_w[l], w_pack, hgrn_lb_logits, *norms, tm=bd * t_new)
        per_seq = lambda a: a.reshape(bd, t_new, a.shape[-1])
        hg = pad_rows(per_seq(o['hg']), HGRN_CHUNK)
        oa, st = _hgrn(hg, state_hgrn[l], HGRN_CHUNK, f"hgrn_sample_l{l}")
        oa = oa[:, :t_new].reshape(1, bd * t_new, A_W)
        iq_rows = o['iqb'].reshape(bd, t_new * IDX_HEADS, IDX_D)
        w_rows = jnp.broadcast_to(jnp.swapaxes(o['iwt'][0], 0, 1).reshape(bd, t_new * IDX_HEADS, 1),
                                  (bd, t_new * IDX_HEADS, LANES))
        keys, thr = _sample_index(l, idx_cache, page_table, iq_rows, w_rows, new_page(o['ikt']), group,
                                  min(TOPK_MAX, (past + t_new) // 4))
        lane = jnp.arange(B_W)
        r8 = jnp.arange(8)
        d_mask = (lane[None, :] // B_HD == r8[:, None])
        c_mask = (lane[None, :] // C_QK == r8[:, None])
        qd = jnp.where(d_mask[None, None], per_seq(o['bqb'])[:, :, None, :], 0).reshape(bd, 8 * t_new, B_W)
        qc = jnp.where(c_mask[None, None], per_seq(o['cqb'])[:, :, None, :], 0).reshape(bd, 8 * t_new, C_W)
        new_kv = [new_page(o[k]) for k in ('bkt', 'bvt32', 'ckt', 'cvt32')]
        ob, oc = _sample_attn(l, caches, page_table, keys, thr, qd, qc, new_kv, diff_lambda[l], group)
        ob = ob[:, :t_new].reshape(1, bd * t_new, B_W)
        oc = oc[:, :t_new].reshape(1, bd * t_new, C_W)
        xs = _outproj(l, xs, oa, ob, oc, o['gate'], w_out[l], hgrn_onorm_w[l], diff_subln_w[l], bd * t_new)
        outs['s_bk'].append(tok_major(o['bkt'], B_HEADS, B_HD).reshape(bd, t_new, B_HEADS, B_HD))
        outs['s_bv'].append(tok_major(o['bvt32'], B_HEADS, B_HD).reshape(bd, t_new, B_HEADS, B_HD))
        outs['s_ik'].append(tok_major(o['ikt'], IDX_D).reshape(bd, t_new, IDX_D))
        outs['s_ck'].append(tok_major(o['ckt'], C_HEADS, 2, C_QK).reshape(bd, t_new, C_HEADS, 2, C_QK))
        outs['s_cv'].append(tok_major(o['cvt32'], C_HEADS, C_DV).reshape(bd, t_new, C_HEADS, C_DV))
        outs['s_st'].append(st)

    stk = {k: jnp.stack(v) for k, v in outs.items()}
    return (xp, xs.reshape(bd, t_new, d),
            stk['p_bk'], stk['p_bv'], stk['p_ik'], stk['p_ck'], stk['p_cv'], stk['p_st'],
            stk['s_bk'], stk['s_bv'], stk['s_ik'], stk['s_ck'], stk['s_cv'], stk['s_st'])
```

```python
import functools
import math

import numpy as np
import jax
import jax.numpy as jnp
from jax import lax
from jax.experimental import pallas as pl
from jax.experimental.pallas import tpu as pltpu

F32 = jnp.float32
BF16 = jnp.bfloat16
I32 = jnp.int32

D_MODEL = 1024
A_W, A_HEADS, A_DK = 256, 4, 64
B_W, B_HEADS, B_HD = 384, 6, 64
IDX_HEADS, IDX_D = 8, 64
C_W, C_HEADS, C_DV, C_QK = 384, 4, 96, 48
TOPK_MAX = 256
PAGE_SIZE = 128
ROPE_THETA = 500000.0
EPS = 1e-6
NEG_BIG = -1e30
LOG2E = math.log2(math.e)
HGRN_CHUNK = 64
LANES = 128
INT_MIN = -(2 ** 31)
VMEM_LIMIT = 56 * 1024 * 1024
PROMPT_TILE = 256
SAMPLE_PAGE_GROUP = 16
HGRN_SEQS_PER_STEP = 2

_OFF_A = 0
_OFF_BQ, _OFF_BK, _OFF_BV, _OFF_BG = 1024, 1408, 1792, 2176
_OFF_IQ, _OFF_IK, _OFF_IW = 2560, 3072, 3200
_OFF_CQ, _OFF_CK, _OFF_CV, _OFF_CG = 3328, 3712, 4096, 4480
_N_PACK = 4864


def _dot(a, b):
    return jnp.dot(a, b, preferred_element_type=F32)


def _dot_nt(a, b):
    return lax.dot_general(a, b, (((1,), (1,)), ((), ())), preferred_element_type=F32)


def _pack_w_in(w):
    i_k = w[:, 3072:3136]
    i_w = w[:, 3136:3144]
    pad = jnp.zeros((w.shape[0], LANES - IDX_HEADS), w.dtype)
    packed = jnp.concatenate([w[:, :3072], i_k, i_k, i_w, pad, w[:, 3144:]], axis=1).astype(BF16)
    wv_t = jnp.concatenate([w[:, 1792:2176], w[:, 3912:4296]], axis=1).T.astype(BF16)
    return packed, wv_t


def _rope_tables(pos, d, n_rep):
    rot = d // 4
    half = rot // 2
    inv = jnp.power(ROPE_THETA, -2.0 * jnp.arange(half, dtype=F32) / rot)
    ang = pos.astype(F32)[:, None] * inv[None, :]
    cos, sin = jnp.cos(ang), jnp.sin(ang)
    t = pos.shape[0]
    one = jnp.ones((t, d - rot), F32)
    zero = jnp.zeros((t, d - half), F32)
    c = jnp.concatenate([cos, cos, one], axis=1)
    sa = jnp.concatenate([-sin, zero], axis=1)
    sb = jnp.concatenate([jnp.zeros((t, half), F32), sin, jnp.zeros((t, d - rot), F32)], axis=1)
    return jnp.stack([jnp.tile(c, (1, n_rep)), jnp.tile(sa, (1, n_rep)), jnp.tile(sb, (1, n_rep))])


def _block_diag_ones(width, group):
    g = np.arange(width) // group
    return jnp.asarray((g[:, None] == g[None, :]).astype(np.float32), dtype=BF16)


def _pad_perm():
    p = np.zeros((C_W, C_HEADS * LANES), np.float32)
    for j in range(C_W):
        p[j, (j // C_DV) * LANES + (j % C_DV)] = 1.0
    return jnp.asarray(p, dtype=BF16)


def _rope(x, tab_ref, width, half):
    c = tab_ref[0, :, :width]
    sa = tab_ref[1, :, :width]
    sb = tab_ref[2, :, :width]
    return x * c + pltpu.roll(x, width - half, 1) * sa + pltpu.roll(x, half, 1) * sb


def _head_rms(x, m_ref, group, w_row):
    ms = _dot((x * x).astype(BF16), m_ref[...]) * (1.0 / group)
    return x * lax.rsqrt(ms + EPS) * w_row


def _inproj_kernel(layer, x_ref, nw_ref, w_ref, wvt_ref, lbl_ref, t64_ref, t48_ref, m64_ref, m48_ref, ppad_ref,
                   bqn_ref, bkn_ref, cqn_ref, ckn_ref,
                   hg_ref, gate_ref, bkt_ref, bvt32_ref, ikt_ref, ckt_ref, cvt32_ref,
                   bqb_ref, bkb_ref, bvt_ref, iqb_ref, ikd_ref, iwt_ref, cqb_ref, cqp_ref, ckp_ref, cvt_ref):
    xf = x_ref[0]
    y = xf * lax.rsqrt(jnp.mean(xf * xf, axis=-1, keepdims=True) + EPS)
    h = (y * nw_ref[...]).astype(BF16)

    def seg(a, b):
        return _dot(h, w_ref[:, a:b])

    lg = lbl_ref[...]
    e = jnp.exp(lg - jnp.max(lg, axis=0, keepdims=True))
    soft = e / jnp.sum(e, axis=0, keepdims=True)
    lb = jnp.zeros((1, A_W), F32)
    for i in range(1, layer + 1):
        lb = lb + soft[i:i + 1, :]
    fz = seg(_OFF_A + 256, _OFF_A + 512)
    en = jnp.exp(-jnp.abs(fz))
    r = 1.0 / (1.0 + en)
    pos_side = fz >= 0
    sig_p = jnp.where(pos_side, r, en * r)
    sig_n = jnp.where(pos_side, en * r, r)
    hg_ref[0, :, 0:256] = seg(_OFF_A, _OFF_A + 256)
    hg_ref[0, :, 256:512] = (1.0 - lb) * sig_n
    hg_ref[0, :, 512:768] = jnp.log(lb + (1.0 - lb) * sig_p)
    hg_ref[0, :, 768:1024] = seg(_OFF_A + 512, _OFF_A + 768)

    def silu(g):
        return g / (1.0 + jnp.exp(-g))

    gate_ref[0, :, 0:256] = silu(seg(_OFF_A + 768, _OFF_A + 1024))
    gate_ref[0, :, 256:640] = silu(seg(_OFF_BG, _OFF_BG + B_W))
    gate_ref[0, :, 640:1024] = silu(seg(_OFF_CG, _OFF_CG + C_W))

    bq = _rope(_head_rms(seg(_OFF_BQ, _OFF_BQ + B_W), m64_ref, B_HD, bqn_ref[...]), t64_ref, B_W, B_HD // 8)
    bqb_ref[0] = (bq * (B_HD ** -0.5 * LOG2E)).astype(BF16)
    bk = _rope(_head_rms(seg(_OFF_BK, _OFF_BK + B_W), m64_ref, B_HD, bkn_ref[...]), t64_ref, B_W, B_HD // 8)
    bkt_ref[0] = bk.T
    bkb_ref[0] = bk.astype(BF16)
    bvt = _dot_nt(wvt_ref[0:B_W, :], h)
    bvt32_ref[0] = bvt
    bvt_ref[0, 0] = bvt.astype(BF16)

    iq = _rope(seg(_OFF_IQ, _OFF_IQ + IDX_HEADS * IDX_D), t64_ref, IDX_HEADS * IDX_D, IDX_D // 8)
    iqb_ref[0] = (iq * (IDX_D ** -0.5)).astype(BF16)
    ikd = _rope(seg(_OFF_IK, _OFF_IK + LANES), t64_ref, LANES, IDX_D // 8)
    ikt_ref[0] = ikd.T[0:IDX_D, :]
    ikd_ref[0] = ikd.astype(BF16)
    iw = seg(_OFF_IW, _OFF_IW + LANES) * (IDX_HEADS ** -0.5)
    iwt_ref[0] = iw.T[0:IDX_HEADS, :]

    cq = _rope(_head_rms(seg(_OFF_CQ, _OFF_CQ + C_W), m48_ref, C_QK, cqn_ref[...]), t48_ref, C_W, C_QK // 8)
    cqb = (cq * (C_QK ** -0.5 * LOG2E)).astype(BF16)
    cqb_ref[0] = cqb
    cqp_ref[0] = _dot(cqb, ppad_ref[...]).astype(BF16)
    ck = _rope(_head_rms(seg(_OFF_CK, _OFF_CK + C_W), m48_ref, C_QK, ckn_ref[...]), t48_ref, C_W, C_QK // 8)
    ckt_ref[0] = ck.T
    ckp_ref[0] = _dot(ck.astype(BF16), ppad_ref[...]).astype(BF16)
    cvt = _dot_nt(wvt_ref[B_W:B_W + C_W, :], h)
    cvt32_ref[0] = cvt
    cvt_ref[0, 0] = cvt.astype(BF16)


def _inproj(layer, x, pos, norm_w, w_packs, lb_logits, bqn, bkn, cqn, ckn, tm):
    w_pack, wv_t = w_packs
    bn, t, d = x.shape
    nt = t // tm
    t64 = _rope_tables(pos, IDX_D, IDX_HEADS)
    t48 = _rope_tables(pos, C_QK, 2 * C_HEADS)
    m64 = _block_diag_ones(B_W, B_HD)
    m48 = _block_diag_ones(C_W, C_QK)
    ppad = _pad_perm()

    def tok(width, dtype):
        return jax.ShapeDtypeStruct((bn, t, width), dtype), pl.BlockSpec((1, tm, width), lambda i, b: (b, i, 0))

    def trf(rows):
        return jax.ShapeDtypeStruct((bn, rows, t), F32), pl.BlockSpec((1, rows, tm), lambda i, b: (b, 0, i))

    def tr(rows, dtype):
        return (jax.ShapeDtypeStruct((bn, nt, rows, tm), dtype),
                pl.BlockSpec((1, 1, rows, tm), lambda i, b: (b, i, 0, 0)))

    outs = dict(
        hg=tok(1024, F32), gate=tok(1024, F32), bkt=trf(B_W), bvt32=trf(B_W), ikt=trf(IDX_D), ckt=trf(C_W), cvt32=trf(C_W),
        bqb=tok(B_W, BF16), bkb=tok(B_W, BF16), bvt=tr(B_W, BF16), iqb=tok(IDX_HEADS * IDX_D, BF16),
        ikd=tok(LANES, BF16),
        iwt=(jax.ShapeDtypeStruct((bn, IDX_HEADS, t), F32), pl.BlockSpec((1, IDX_HEADS, tm), lambda i, b: (b, 0, i))),
        cqb=tok(C_W, BF16), cqp=tok(C_HEADS * LANES, BF16), ckp=tok(C_HEADS * LANES, BF16), cvt=tr(C_W, BF16),
    )
    names = list(outs)

    def const(shape):
        nd = len(shape)
        return pl.BlockSpec(shape, lambda i, b: (0,) * nd)

    in_specs = [
        pl.BlockSpec((1, tm, d), lambda i, b: (b, i, 0)),
        const((1, d)),
        const((d, _N_PACK)),
        const((B_W + C_W, d)),
        const(lb_logits.shape),
        pl.BlockSpec((3, tm, IDX_HEADS * IDX_D), lambda i, b: (0, i, 0)),
        pl.BlockSpec((3, tm, C_W), lambda i, b: (0, i, 0)),
        const((B_W, B_W)), const((C_W, C_W)), const((C_W, C_HEADS * LANES)),
        const((1, B_W)), const((1, B_W)), const((1, C_W)), const((1, C_W)),
    ]
    res = pl.pallas_call(
        functools.partial(_inproj_kernel, layer),
        grid=(nt, bn),
        in_specs=in_specs,
        out_specs=[outs[n][1] for n in names],
        out_shape=[outs[n][0] for n in names],
        compiler_params=pltpu.CompilerParams(dimension_semantics=("arbitrary", "arbitrary"),
                                             vmem_limit_bytes=VMEM_LIMIT),
        name=f"inproj_l{layer}_t{t}",
    )(x, norm_w.reshape(1, d), w_pack, wv_t, lb_logits, t64, t48, m64, m48, ppad,
      jnp.tile(bqn, B_HEADS).reshape(1, B_W), jnp.tile(bkn, B_HEADS).reshape(1, B_W),
      jnp.tile(cqn, 2 * C_HEADS).reshape(1, C_W), jnp.tile(ckn, 2 * C_HEADS).reshape(1, C_W))
    return dict(zip(names, res))


_HGRN_LEVELS = (1, 2, 4, 8, 16, 32)


def _hgrn_tables():
    c = HGRN_CHUNK
    t = np.arange(c)[:, None]
    u = np.arange(c)[None, :]
    mats = [(u <= t), (u > t)]
    masks = [(t == u)]
    for h in _HGRN_LEVELS:
        r = (t // (2 * h)) * (2 * h) + h - 1
        right = (t % (2 * h)) >= h
        mats.append(np.where(right, (u > r) & (u <= t), (u > t) & (u <= r)))
        masks.append((t // (2 * h) == u // (2 * h)) & right & ((u % (2 * h)) < h))
    tall = np.tile(np.concatenate(mats, axis=0).astype(np.float32), (1, 3))
    mk = np.stack([np.tile(m, (1, 2)) for m in masks]).astype(np.float32)
    return jnp.asarray(tall, dtype=BF16), jnp.asarray(mk)


def _pair_expand(a):
    lane = lax.broadcasted_iota(I32, a.shape, 1)
    zero = jnp.zeros_like(a)
    return jnp.concatenate([jnp.where(lane < A_DK, a, zero), jnp.where(lane >= A_DK, a, zero)], axis=0)


def _hgrn_kernel(nchunk, nseq, hg_ref, tall_ref, mk_ref, s0_ref, o_ref, sf_ref, st_ref):
    c = HGRN_CHUNK

    @pl.when(pl.program_id(1) == 0)
    def _():
        st_ref[...] = s0_ref[...]

    row = lax.broadcasted_iota(I32, (2 * A_DK, 2 * A_DK), 0)
    col = lax.broadcasted_iota(I32, (2 * A_DK, 2 * A_DK), 1)
    same_head = (row // A_DK) == (col // A_DK)
    zpad = jnp.zeros((c, 2 * A_DK), BF16)

    units = [(sq, p) for sq in range(nseq) for p in range(2)]
    n_lev = len(_HGRN_LEVELS)
    for ci in range(nchunk):
        rows = slice(ci * c, (ci + 1) * c)
        exs = []
        for sq in range(nseq):
            lf = hg_ref[sq, rows, 512:768]
            l1 = lf.astype(BF16)
            r1 = lf - l1.astype(F32)
            l2 = r1.astype(BF16)
            l3 = (r1 - l2.astype(F32)).astype(BF16)
            exs.append(_dot(tall_ref[...], jnp.concatenate([l1, l2, l3], axis=0)))
        exs = [jnp.exp(e) for e in exs]
        inter, level, upd, vexp, states = [], [], [], [], []
        for sq, p in units:
            ex = exs[sq]
            lanes = slice(p * 128, (p + 1) * 128)
            qp = hg_ref[sq, rows, lanes]
            kp = hg_ref[sq, rows, 256 + p * 128:256 + (p + 1) * 128]
            vp = hg_ref[sq, rows, 768 + p * 128:768 + (p + 1) * 128].astype(BF16)
            st = st_ref[sq, p]
            states.append(st)
            vexp.append(_pair_expand(vp))
            inter.append(_dot_nt((qp * ex[0:c, lanes]).astype(BF16), st.astype(BF16)))
            lv = [_dot_nt(qp.astype(BF16), _pair_expand(kp.astype(BF16)))]
            for li in range(n_lev):
                xl = ex[(2 + li) * c:(3 + li) * c, lanes]
                lv.append(_dot_nt((qp * xl).astype(BF16), _pair_expand((kp * xl).astype(BF16))))
            level.append(lv)
            kh = (kp * ex[c:2 * c, lanes]).astype(BF16)
            vt = jnp.concatenate([vp, zpad], axis=0).astype(F32).T.astype(BF16)
            upd.append(_dot(vt, jnp.concatenate([kh, zpad], axis=0)))
        scores = []
        for lv in level:
            sc = mk_ref[0] * lv[0]
            for li in range(n_lev):
                sc = sc + mk_ref[1 + li] * lv[1 + li]
            scores.append(sc.astype(BF16))
        intra = [_dot(sc, ve) for sc, ve in zip(scores, vexp)]
        for u, (sq, p) in enumerate(units):
            lanes = slice(p * 128, (p + 1) * 128)
            o_ref[sq, rows, lanes] = inter[u] + intra[u]
            decay = exs[sq][c - 1:c, lanes]
            st_ref[sq, p] = jnp.where(same_head, states[u] * decay + upd[u], 0.0)
    sf_ref[...] = st_ref[...]


def _state_to_pairs(s):
    bn = s.shape[0]
    st = jnp.swapaxes(s, -1, -2).reshape(bn, 2, 2, A_DK, A_DK)
    z = jnp.zeros((bn, 2, A_DK, A_DK), s.dtype)
    top = jnp.concatenate([st[:, :, 0], z], axis=-1)
    bot = jnp.concatenate([z, st[:, :, 1]], axis=-1)
    return jnp.concatenate([top, bot], axis=-2)


def _pairs_to_state(sp):
    bn = sp.shape[0]
    h0 = sp[:, :, :A_DK, :A_DK]
    h1 = sp[:, :, A_DK:, A_DK:]
    return jnp.swapaxes(jnp.stack([h0, h1], axis=2).reshape(bn, A_HEADS, A_DK, A_DK), -1, -2)


def _hgrn(hg, s0, tb, name):
    bn, t, _ = hg.shape
    nseq = HGRN_SEQS_PER_STEP
    assert bn % nseq == 0
    tall, mk = _hgrn_tables()
    o, sf = pl.pallas_call(
        functools.partial(_hgrn_kernel, tb // HGRN_CHUNK, nseq),
        grid=(bn // nseq, t // tb),
        in_specs=[pl.BlockSpec((nseq, tb, 1024), lambda b, j: (b, j, 0)),
                  pl.BlockSpec(tall.shape, lambda b, j: (0, 0)),
                  pl.BlockSpec(mk.shape, lambda b, j: (0, 0, 0)),
                  pl.BlockSpec((nseq, 2, 128, 128), lambda b, j: (b, 0, 0, 0))],
        out_specs=[pl.BlockSpec((nseq, tb, A_W), lambda b, j: (b, j, 0)),
                   pl.BlockSpec((nseq, 2, 128, 128), lambda b, j: (b, 0, 0, 0))],
        out_shape=[jax.ShapeDtypeStruct((bn, t, A_W), F32), jax.ShapeDtypeStruct((bn, 2, 128, 128), F32)],
        scratch_shapes=[pltpu.VMEM((nseq, 2, 128, 128), F32)],
        compiler_params=pltpu.CompilerParams(dimension_semantics=("arbitrary", "arbitrary"),
                                             vmem_limit_bytes=VMEM_LIMIT),
        name=name,
    )(hg, tall, mk, _state_to_pairs(s0))
    return o, _pairs_to_state(sf)


DSA_TQ = 128


def _key_of(score):
    bits = lax.bitcast_convert_type(score, I32)
    key = bits ^ ((bits >> 31) & 0x7FFFFFFF)
    return jnp.where(score == 0.0, 0, key)


def _count_rows(pred_fn, nk, ch, width):
    def body(kc, acc):
        hit = jnp.where(pred_fn(kc), 1, 0)
        return acc + jnp.sum(hit.reshape(ch // 8, 8, width), axis=0)

    acc = lax.fori_loop(0, nk, body, jnp.zeros((8, width), I32))
    return jnp.sum(acc, axis=0, keepdims=True)


def _select_topk(keys_ref, nk, ch, width, n_keep, pos_bits):
    def chunk(kc):
        return keys_ref[pl.ds(kc * ch, ch), :]

    def kpos(kc):
        return kc * ch + lax.broadcasted_iota(I32, (ch, width), 0)

    def bit_step(i, tu):
        cand = tu | jnp.left_shift(jnp.int32(1), 31 - i)
        cnt = _count_rows(lambda kc: chunk(kc) >= (cand ^ INT_MIN), nk, ch, width)
        return jnp.where(cnt >= n_keep, cand, tu)

    thr = lax.fori_loop(0, 32, bit_step, jnp.zeros((1, width), I32)) ^ INT_MIN
    cnt_ge = _count_rows(lambda kc: chunk(kc) >= thr, nk, ch, width)
    excess = jnp.where(thr > INT_MIN, cnt_ge - n_keep, 0)

    @pl.when(jnp.max(excess) > 0)
    def _():
        need = n_keep - _count_rows(lambda kc: chunk(kc) > thr, nk, ch, width)

        def pos_step(i, c):
            cand = c | jnp.left_shift(jnp.int32(1), pos_bits - 1 - i)
            before = _count_rows(lambda kc: (chunk(kc) == thr) & (kpos(kc) < cand), nk, ch, width)
            return jnp.where(before <= need - 1, cand, c)

        cut = lax.fori_loop(0, pos_bits, pos_step, jnp.zeros((1, width), I32))

        def drop(kc, carry):
            k = chunk(kc)
            late_tie = (k == thr) & (kpos(kc) > cut) & (excess > 0)
            keys_ref[pl.ds(kc * ch, ch), :] = jnp.where(late_tie, INT_MIN, k)
            return carry

        lax.fori_loop(0, nk, drop, 0)

    return jnp.maximum(thr, INT_MIN + 1)


def _lane_halves(x):
    lane = lax.broadcasted_iota(I32, x.shape, 1)
    zero = jnp.zeros_like(x)
    return jnp.concatenate([jnp.where(lane < B_HD, x, zero), jnp.where(lane >= B_HD, x, zero)], axis=0)


def _softmax_updates_t(scores, values_t, m_ref, l_ref, acc_ref):
    alphas, probs = [], []
    for j, s in enumerate(scores):
        m_old = m_ref[j, 0:1, :]
        m_new = jnp.maximum(m_old, jnp.max(s, axis=0, keepdims=True))
        alpha = jnp.exp2(m_old - m_new)
        e = jnp.exp2(s - m_new)
        l_ref[j, 0:1, :] = alpha * l_ref[j, 0:1, :] + jnp.sum(e, axis=0, keepdims=True)
        m_ref[j, 0:1, :] = m_new
        alphas.append(alpha)
        probs.append(e.astype(BF16))
    updates = [_dot(v_t, p) for v_t, p in zip(values_t, probs)]
    for j, (alpha, upd) in enumerate(zip(alphas, updates)):
        acc_ref[j] = alpha * acc_ref[j] + upd


def _dsa_prompt_kernel(n_keep, ch, pos_bits, iqb_ref, iwt_ref, ikd_ref, bqb_ref, bkb_ref, bvt_ref, o_ref,
                       keys_ref, m_ref, l_ref, acc_ref):
    tq = DSA_TQ
    qb = pl.program_id(1)
    nk = (qb * tq + tq + ch - 1) // ch
    qpos = qb * tq + lax.broadcasted_iota(I32, (1, tq), 1)

    iq = iqb_ref[0]
    iq_rows = jnp.concatenate([_lane_halves(iq[:, j * LANES:(j + 1) * LANES]) for j in range(IDX_HEADS // 2)], axis=0)
    wt = iwt_ref[0]

    def index_chunk(kc, carry):
        hc = ch // 2
        raw = [_dot_nt(ikd_ref[0, pl.ds(kc * ch + i * hc, hc), :], iq_rows) for i in range(2)]
        for i, s in enumerate(raw):
            acc = jnp.zeros((hc, tq), F32)
            for h in range(IDX_HEADS):
                acc = acc + wt[h:h + 1, :] * jnp.maximum(s[:, h * tq:(h + 1) * tq], 0.0)
            kp = kc * ch + i * hc + lax.broadcasted_iota(I32, (hc, tq), 0)
            keys_ref[pl.ds(kc * ch + i * hc, hc), :] = jnp.where(kp <= qpos, _key_of(acc), INT_MIN)
        return carry

    lax.fori_loop(0, nk, index_chunk, 0)

    @pl.when(nk % 2 == 1)
    def _():
        keys_ref[pl.ds(nk * ch, ch), :] = jnp.full((ch, tq), INT_MIN, I32)

    thr = _select_topk(keys_ref, (nk + 1) // 2, 2 * ch, tq, n_keep, pos_bits)

    n_pairs = B_HEADS // 2
    bq = bqb_ref[0]
    q_rows = [_lane_halves(bq[:, p * LANES:(p + 1) * LANES]) for p in range(n_pairs)]
    m_ref[...] = jnp.full(m_ref.shape, NEG_BIG, F32)
    l_ref[...] = jnp.zeros(l_ref.shape, F32)
    acc_ref[...] = jnp.zeros(acc_ref.shape, F32)

    def attend(kc, carry):
        sel = keys_ref[pl.ds(kc * ch, ch), :] >= thr
        raw = [_dot_nt(bkb_ref[0, pl.ds(kc * ch, ch), p * LANES:(p + 1) * LANES], q_rows[p])
               for p in range(n_pairs)]
        scores = [jnp.concatenate([jnp.where(sel, s[:, :tq], NEG_BIG), jnp.where(sel, s[:, tq:], NEG_BIG)], axis=1)
                  for s in raw]
        values_t = [bvt_ref[0, kc, p * LANES:(p + 1) * LANES, :] for p in range(n_pairs)]
        _softmax_updates_t(scores, values_t, m_ref, l_ref, acc_ref)
        return carry

    lax.fori_loop(0, nk, attend, 0)
    outs = []
    for p in range(n_pairs):
        inv = 1.0 / l_ref[p, 0:1, :]
        outs.append(acc_ref[p, 0:B_HD, 0:tq] * inv[:, 0:tq])
        outs.append(acc_ref[p, B_HD:, tq:] * inv[:, tq:])
    o_ref[0] = jnp.concatenate(outs, axis=0).T


def _dsa_prompt(o, ch):
    bn, s, _ = o['bkb'].shape
    n_keep = min(TOPK_MAX, s // 4)
    nb = s // DSA_TQ
    assert (s // ch) % 2 == 0, "the threshold search walks key chunks in pairs"
    n_pairs = B_HEADS // 2
    return pl.pallas_call(
        functools.partial(_dsa_prompt_kernel, n_keep, ch, s.bit_length()),
        grid=(bn, nb),
        in_specs=[pl.BlockSpec((1, DSA_TQ, IDX_HEADS * IDX_D), lambda b, q: (b, q, 0)),
                  pl.BlockSpec((1, IDX_HEADS, DSA_TQ), lambda b, q: (b, 0, q)),
                  pl.BlockSpec((1, s, LANES), lambda b, q: (b, 0, 0)),
                  pl.BlockSpec((1, DSA_TQ, B_W), lambda b, q: (b, q, 0)),
                  pl.BlockSpec((1, s, B_W), lambda b, q: (b, 0, 0)),
                  pl.BlockSpec((1, s // ch, B_W, ch), lambda b, q: (b, 0, 0, 0))],
        out_specs=pl.BlockSpec((1, DSA_TQ, B_W), lambda b, q: (b, q, 0)),
        out_shape=jax.ShapeDtypeStruct((bn, s, B_W), F32),
        scratch_shapes=[pltpu.VMEM((s, DSA_TQ), I32), pltpu.VMEM((n_pairs, 8, 2 * DSA_TQ), F32),
                        pltpu.VMEM((n_pairs, 8, 2 * DSA_TQ), F32), pltpu.VMEM((n_pairs, LANES, 2 * DSA_TQ), F32)],
        compiler_params=pltpu.CompilerParams(dimension_semantics=("arbitrary", "arbitrary"),
                                             vmem_limit_bytes=VMEM_LIMIT),
        name="dsa_prompt",
    )(o['iqb'], o['iwt'], o['ikd'], o['bqb'], o['bkb'], o['bvt'])


def _diff_lambda(dl, lam_init):
    a = jnp.sum(dl[0:1, :] * dl[1:2, :], axis=1, keepdims=True)
    b = jnp.sum(dl[2:3, :] * dl[3:4, :], axis=1, keepdims=True)
    return jnp.exp(a) - jnp.exp(b) + lam_init


def _diff_prompt_kernel(lam_init, ch, dl_ref, cqp_ref, ckp_ref, cvt_ref, o_ref, m_ref, l_ref, acc_ref):
    tq = DSA_TQ
    qb = pl.program_id(1)
    nk = (qb * tq + tq + ch - 1) // ch
    n_full = (qb * tq) // ch
    qpos = qb * tq + lax.broadcasted_iota(I32, (1, tq), 1)
    lam = _diff_lambda(dl_ref[...], lam_init)
    cq = cqp_ref[0]
    lane = lax.broadcasted_iota(I32, (tq, LANES), 1)
    q_rows = []
    for h in range(C_HEADS):
        qh = cq[:, h * LANES:(h + 1) * LANES]
        zero = jnp.zeros_like(qh)
        q_rows.append(jnp.concatenate([jnp.where(lane < C_QK, qh, zero),
                                       jnp.where((lane >= C_QK) & (lane < 2 * C_QK), qh, zero)], axis=0))
    m_ref[...] = jnp.full(m_ref.shape, NEG_BIG, F32)
    l_ref[...] = jnp.zeros(l_ref.shape, F32)
    acc_ref[...] = jnp.zeros(acc_ref.shape, F32)

    def attend(kc, masked):
        scores = [_dot_nt(ckp_ref[0, pl.ds(kc * ch, ch), h * LANES:(h + 1) * LANES], q_rows[h])
                  for h in range(C_HEADS)]
        if masked:
            vis = (kc * ch + lax.broadcasted_iota(I32, (ch, tq), 0)) <= qpos
            scores = [jnp.concatenate([jnp.where(vis, s[:, :tq], NEG_BIG), jnp.where(vis, s[:, tq:], NEG_BIG)],
                                      axis=1) for s in scores]
        values_t = [cvt_ref[0, kc, h * C_DV:(h + 1) * C_DV, :] for h in range(C_HEADS)]
        _softmax_updates_t(scores, values_t, m_ref, l_ref, acc_ref)

    def full_chunk(kc, carry):
        attend(kc, False)
        return carry

    def diag_chunk(kc, carry):
        attend(kc, True)
        return carry

    lax.fori_loop(0, n_full, full_chunk, 0)
    lax.fori_loop(n_full, nk, diag_chunk, 0)
    outs = []
    for h in range(C_HEADS):
        inv = 1.0 / l_ref[h, 0:1, :]
        acc = acc_ref[h]
        outs.append(acc[:, :tq] * inv[:, :tq] - lam * (acc[:, tq:] * inv[:, tq:]))
    o_ref[0] = jnp.concatenate(outs, axis=0).T


def _diff_prompt(layer, o, dl, ch):
    bn, s, _ = o['ckp'].shape
    lam_init = 0.8 - 0.6 * math.exp(-0.3 * layer)
    return pl.pallas_call(
        functools.partial(_diff_prompt_kernel, lam_init, ch),
        grid=(bn, s // DSA_TQ),
        in_specs=[pl.BlockSpec((4, C_QK), lambda b, q: (0, 0)),
                  pl.BlockSpec((1, DSA_TQ, C_HEADS * LANES), lambda b, q: (b, q, 0)),
                  pl.BlockSpec((1, s, C_HEADS * LANES), lambda b, q: (b, 0, 0)),
                  pl.BlockSpec((1, s // ch, C_W, ch), lambda b, q: (b, 0, 0, 0))],
        out_specs=pl.BlockSpec((1, DSA_TQ, C_W), lambda b, q: (b, q, 0)),
        out_shape=jax.ShapeDtypeStruct((bn, s, C_W), F32),
        scratch_shapes=[pltpu.VMEM((C_HEADS, 8, 2 * DSA_TQ), F32), pltpu.VMEM((C_HEADS, 8, 2 * DSA_TQ), F32),
                        pltpu.VMEM((C_HEADS, C_DV, 2 * DSA_TQ), F32)],
        compiler_params=pltpu.CompilerParams(dimension_semantics=("arbitrary", "arbitrary"),
                                             vmem_limit_bytes=VMEM_LIMIT),
        name="diff_prompt",
    )(dl, o['cqp'], o['ckp'], o['cvt'])


def _token_minor(cache):
    nd = cache.ndim
    c = jnp.transpose(cache, (0, 1) + tuple(range(3, nd)) + (2,))
    return c.reshape(c.shape[0], c.shape[1], -1, c.shape[-1])


def _page_specs(layer, n, group, width):
    def spec(j):
        return pl.BlockSpec((None, None, width, PAGE_SIZE), lambda b, g, pt: (layer, pt[b, g * group + j], 0, 0))

    return [spec(j) for _ in range(n) for j in range(group)]


def _rows_of_queries(k4, t_new):
    return jnp.concatenate([jnp.broadcast_to(k4[t:t + 1, :], (8, k4.shape[1])) for t in range(t_new)], axis=0)


def _select_topk_rows(keys_ref, lp, n_keep, pos_bits):
    cw = 16 * LANES

    def count(pred):
        acc = jnp.zeros((8, LANES), I32)
        for c0 in range(0, lp, cw):
            w = min(cw, lp - c0)
            hit = jnp.where(pred(keys_ref[0, :, c0:c0 + w], c0, w), 1, 0)
            for j in range(w // LANES):
                acc = acc + hit[:, j * LANES:(j + 1) * LANES]
        return jnp.sum(acc.astype(F32), axis=1, keepdims=True).astype(I32)

    def kpos(c0, w):
        return c0 + lax.broadcasted_iota(I32, (8, w), 1)

    def bit_step(i, tu):
        cand = tu | jnp.left_shift(jnp.int32(1), 31 - i)
        cnt = count(lambda k, c0, w: k >= (cand ^ INT_MIN))
        return jnp.where(cnt >= n_keep, cand, tu)

    thr = lax.fori_loop(0, 32, bit_step, jnp.zeros((8, 1), I32)) ^ INT_MIN
    excess = jnp.where(thr > INT_MIN, count(lambda k, c0, w: k >= thr) - n_keep, 0)

    @pl.when(jnp.max(excess) > 0)
    def _():
        need = n_keep - count(lambda k, c0, w: k > thr)

        def pos_step(i, c):
            cand = c | jnp.left_shift(jnp.int32(1), pos_bits - 1 - i)
            before = count(lambda k, c0, w: (k == thr) & (kpos(c0, w) < cand))
            return jnp.where(before <= need - 1, cand, c)

        cut = lax.fori_loop(0, pos_bits, pos_step, jnp.zeros((8, 1), I32))
        for c0 in range(0, lp, cw):
            w = min(cw, lp - c0)
            k = keys_ref[0, :, c0:c0 + w]
            late_tie = (k == thr) & (kpos(c0, w) > cut) & (excess > 0)
            keys_ref[0, :, c0:c0 + w] = jnp.where(late_tie, INT_MIN, k)

    return jnp.maximum(thr, INT_MIN + 1)


def _sample_index_kernel(group, n_groups, t_new, n_keep, pos_bits, pt_ref, *refs):
    pages = refs[:group]
    iq_ref, w_ref, iknew_ref, keys_ref, thr_ref = refs[group:]
    g = pl.program_id(1)
    gk = group * PAGE_SIZE
    past = n_groups * gk
    iq = iq_ref[0]
    wcol = w_ref[0][:, 0:1]

    def scores(kmat_t):
        s = jnp.maximum(_dot(iq, kmat_t), 0.0) * wcol
        return jnp.concatenate([jnp.sum(s[8 * t:8 * t + 8, :], axis=0, keepdims=True) for t in range(t_new)], axis=0)

    kcat = jnp.concatenate([p[...].astype(BF16) for p in pages], axis=1)
    key = _key_of(scores(kcat))
    key8 = jnp.concatenate([key] * (8 // t_new), axis=0)
    for gg in range(n_groups):
        @pl.when(g == gg)
        def _(gg=gg):
            keys_ref[0, :, gg * gk:(gg + 1) * gk] = key8

    @pl.when(g == n_groups - 1)
    def _():
        kn = _key_of(scores(iknew_ref[0]))
        row = lax.broadcasted_iota(I32, kn.shape, 0)
        lane = lax.broadcasted_iota(I32, kn.shape, 1)
        kn = jnp.where((lane <= row) & (lane < t_new), kn, INT_MIN)
        keys_ref[0, :, past:past + LANES] = jnp.concatenate([kn] * (8 // t_new), axis=0)
        thr = _select_topk_rows(keys_ref, past + LANES, n_keep, pos_bits)
        thr_ref[0] = jnp.broadcast_to(thr, (8, LANES))


def _sample_index(layer, cache_idx_k, page_table, iq_rows, w_rows, iknew, group, n_keep):
    bd, n_pages = page_table.shape
    n_groups = n_pages // group
    t_new = iq_rows.shape[1] // 8
    past = n_pages * PAGE_SIZE
    lp = past + LANES
    grid_spec = pltpu.PrefetchScalarGridSpec(
        num_scalar_prefetch=1,
        grid=(bd, n_groups),
        in_specs=_page_specs(layer, 1, group, IDX_D) + [
            pl.BlockSpec((1, 8 * t_new, IDX_D), lambda b, g, pt: (b, 0, 0)),
            pl.BlockSpec((1, 8 * t_new, LANES), lambda b, g, pt: (b, 0, 0)),
            pl.BlockSpec((1, IDX_D, PAGE_SIZE), lambda b, g, pt: (b, 0, 0))],
        out_specs=[pl.BlockSpec((1, 8, lp), lambda b, g, pt: (b, 0, 0)),
                   pl.BlockSpec((1, 8, LANES), lambda b, g, pt: (b, 0, 0))],
    )
    return pl.pallas_call(
        functools.partial(_sample_index_kernel, group, n_groups, t_new, n_keep, lp.bit_length()),
        grid_spec=grid_spec,
        out_shape=[jax.ShapeDtypeStruct((bd, 8, lp), I32), jax.ShapeDtypeStruct((bd, 8, LANES), I32)],
        compiler_params=pltpu.CompilerParams(dimension_semantics=("arbitrary", "arbitrary"),
                                             vmem_limit_bytes=VMEM_LIMIT),
        name=f"sample_index_l{layer}",
    )(page_table, *([cache_idx_k] * group), iq_rows, w_rows, iknew)


def _softmax_step(s, valid, v_t, m_ref, l_ref, acc_ref):
    if valid is not None:
        s = jnp.where(valid, s, NEG_BIG)
    m_old = m_ref[:, 0:1]
    m_new = jnp.maximum(m_old, jnp.max(s, axis=1, keepdims=True))
    alpha = jnp.exp2(m_old - m_new)
    e = jnp.exp2(s - m_new)
    if valid is not None:
        e = jnp.where(valid, e, 0.0)
    l_new = alpha * l_ref[:, 0:1] + jnp.sum(e, axis=1, keepdims=True)
    acc_ref[...] = alpha * acc_ref[...] + _dot_nt(e.astype(BF16), v_t)
    m_ref[...] = jnp.broadcast_to(m_new, m_ref.shape)
    l_ref[...] = jnp.broadcast_to(l_new, l_ref.shape)


def _sample_attn_kernel(group, n_groups, t_new, lam_init, pt_ref, *refs):
    n = group
    dk_pages, dv_pages, ck_pages, cv_pages = refs[0:n], refs[n:2 * n], refs[2 * n:3 * n], refs[3 * n:4 * n]
    (keys_ref, keysn_ref, thr_ref, qd_ref, qc_ref, dkn_ref, dvn_ref, ckn_ref, cvn_ref, dl_ref,
     ob_ref, oc_ref, md_ref, ld_ref, ad_ref, mc_ref, lc_ref, ac_ref) = refs[4 * n:]
    g = pl.program_id(1)
    rows = 8 * t_new

    @pl.when(g == 0)
    def _():
        for m_ref, l_ref, a_ref in ((md_ref, ld_ref, ad_ref), (mc_ref, lc_ref, ac_ref)):
            m_ref[...] = jnp.full(m_ref.shape, NEG_BIG, F32)
            l_ref[...] = jnp.zeros(l_ref.shape, F32)
            a_ref[...] = jnp.zeros(a_ref.shape, F32)

    def cat(pages):
        return jnp.concatenate([p[...].astype(BF16) for p in pages], axis=1)

    thr = _rows_of_queries(thr_ref[0], t_new)[:, 0:1]
    qd = qd_ref[0]
    qc = qc_ref[0]
    sel = _rows_of_queries(keys_ref[0], t_new) >= thr
    _softmax_step(_dot(qd, cat(dk_pages)), sel, cat(dv_pages), md_ref, ld_ref, ad_ref)
    _softmax_step(_dot(qc, cat(ck_pages)), None, cat(cv_pages), mc_ref, lc_ref, ac_ref)

    @pl.when(g == n_groups - 1)
    def _():
        row = lax.broadcasted_iota(I32, (rows, LANES), 0)
        lane = lax.broadcasted_iota(I32, (rows, LANES), 1)
        causal = (lane <= row // 8) & (lane < t_new)
        seln = causal & (_rows_of_queries(keysn_ref[0], t_new) >= thr)
        _softmax_step(_dot(qd, dkn_ref[0]), seln, dvn_ref[0], md_ref, ld_ref, ad_ref)
        _softmax_step(_dot(qc, ckn_ref[0]), causal, cvn_ref[0], mc_ref, lc_ref, ac_ref)

        r = lax.broadcasted_iota(I32, (rows, B_W), 0) % 8
        ln = lax.broadcasted_iota(I32, (rows, B_W), 1)
        od = jnp.where(ln // B_HD == r, ad_ref[...] / ld_ref[:, 0:1], 0.0)
        lam = _diff_lambda(dl_ref[...], lam_init)
        coef = jnp.where(r % 2 == 0, 1.0, -lam)
        oc = jnp.where(ln // C_DV == r // 2, coef * (ac_ref[...] / lc_ref[:, 0:1]), 0.0)
        pad = jnp.zeros((8 - t_new, B_W), F32)
        ob_ref[0] = jnp.concatenate([jnp.sum(od[8 * t:8 * t + 8], axis=0, keepdims=True) for t in range(t_new)] + [pad],
                                    axis=0)
        oc_ref[0] = jnp.concatenate([jnp.sum(oc[8 * t:8 * t + 8], axis=0, keepdims=True) for t in range(t_new)] + [pad],
                                    axis=0)


def _sample_attn(layer, caches, page_table, keys, thr, qd, qc, new_kv, dl, group):
    bd, n_pages = page_table.shape
    n_groups = n_pages // group
    rows = qd.shape[1]
    t_new = rows // 8
    gk = group * PAGE_SIZE
    past = n_pages * PAGE_SIZE
    lam_init = 0.8 - 0.6 * math.exp(-0.3 * layer)

    def per_seq(shape):
        nd = len(shape)
        return pl.BlockSpec((1,) + shape, lambda b, g, pt: (b,) + (0,) * nd)

    grid_spec = pltpu.PrefetchScalarGridSpec(
        num_scalar_prefetch=1,
        grid=(bd, n_groups),
        in_specs=_page_specs(layer, 4, group, B_W) + [
            pl.BlockSpec((1, 8, gk), lambda b, g, pt: (b, 0, g)),
            pl.BlockSpec((1, 8, LANES), lambda b, g, pt: (b, 0, past // LANES)),
            per_seq((8, LANES)), per_seq((rows, B_W)), per_seq((rows, C_W)),
            per_seq((B_W, PAGE_SIZE)), per_seq((B_W, PAGE_SIZE)), per_seq((C_W, PAGE_SIZE)), per_seq((C_W, PAGE_SIZE)),
            pl.BlockSpec((4, C_QK), lambda b, g, pt: (0, 0))],
        out_specs=[per_seq((8, B_W)), per_seq((8, C_W))],
        scratch_shapes=[pltpu.VMEM((rows, LANES), F32), pltpu.VMEM((rows, LANES), F32), pltpu.VMEM((rows, B_W), F32),
                        pltpu.VMEM((rows, LANES), F32), pltpu.VMEM((rows, LANES), F32), pltpu.VMEM((rows, C_W), F32)],
    )
    pages = [c for c in caches for _ in range(group)]
    return pl.pallas_call(
        functools.partial(_sample_attn_kernel, group, n_groups, t_new, lam_init),
        grid_spec=grid_spec,
        out_shape=[jax.ShapeDtypeStruct((bd, 8, B_W), F32), jax.ShapeDtypeStruct((bd, 8, C_W), F32)],
        compiler_params=pltpu.CompilerParams(dimension_semantics=("arbitrary", "arbitrary"),
                                             vmem_limit_bytes=VMEM_LIMIT),
        name=f"sample_attn_l{layer}",
    )(page_table, *pages, keys, keys, thr, qd, qc, *new_kv, dl)


def _outproj_kernel(c_scale, x_ref, oa_ref, ob_ref, oc_ref, gate_ref, w_ref, ma_ref, mc_ref, na_ref, nc_ref, y_ref):
    ga = _head_rms(oa_ref[0], ma_ref, A_DK, na_ref[...]) * gate_ref[0, :, 0:A_W]
    gb = ob_ref[0] * gate_ref[0, :, A_W:A_W + B_W]
    gc = _head_rms(oc_ref[0], mc_ref, C_DV, nc_ref[...]) * c_scale * gate_ref[0, :, A_W + B_W:]
    y = x_ref[0] + _dot(ga.astype(BF16), w_ref[0:A_W, :])
    y = y + _dot(gb.astype(BF16), w_ref[A_W:A_W + B_W, :])
    y_ref[0] = y + _dot(gc.astype(BF16), w_ref[A_W + B_W:, :])


def _outproj(layer, x, oa, ob, oc, gate, w_out, onorm_w, subln_w, tm):
    bn, t, d = x.shape
    lam_init = 0.8 - 0.6 * math.exp(-0.3 * layer)

    def tok(width):
        return pl.BlockSpec((1, tm, width), lambda b, i: (b, i, 0))

    def const(shape):
        return pl.BlockSpec(shape, lambda b, i: (0, 0))

    return pl.pallas_call(
        functools.partial(_outproj_kernel, 1.0 - lam_init),
        grid=(bn, t // tm),
        in_specs=[tok(d), tok(A_W), tok(B_W), tok(C_W), tok(d), const((d, d)), const((A_W, A_W)), const((C_W, C_W)),
                  const((1, A_W)), const((1, C_W))],
        out_specs=tok(d),
        out_shape=jax.ShapeDtypeStruct((bn, t, d), F32),
        compiler_params=pltpu.CompilerParams(dimension_semantics=("arbitrary", "arbitrary"),
                                             vmem_limit_bytes=VMEM_LIMIT),
        name=f"outproj_l{layer}_t{t}",
    )(x, oa, ob, oc, gate, w_out.astype(BF16), _block_diag_ones(A_W, A_DK), _block_diag_ones(C_W, C_DV),
      jnp.tile(onorm_w, A_HEADS).reshape(1, A_W), jnp.tile(subln_w, C_HEADS).reshape(1, C_W))


def kernel(x_prompt, x_sample, cache_dsa_k, cache_dsa_v, cache_idx_k, cache_diff_k, cache_diff_v, state_hgrn,
           page_table, norm_w, w_in, w_out, hgrn_lb_logits, hgrn_onorm_w, dsa_qnorm_w, dsa_knorm_w,
           diff_qnorm_w, diff_knorm_w, diff_lambda, diff_subln_w):
    depth = norm_w.shape[0]
    bn, s, d = x_prompt.shape
    bd, t_new, _ = x_sample.shape
    n_pool = cache_dsa_k.shape[1]
    n_pages = page_table.shape[1]
    past = n_pages * PAGE_SIZE
    group = math.gcd(SAMPLE_PAGE_GROUP, n_pages)
    pos_p = jnp.arange(s, dtype=I32)
    pos_s = past + jnp.arange(bd * t_new, dtype=I32) % t_new
    caches = [_token_minor(c) for c in (cache_dsa_k, cache_dsa_v, cache_diff_k, cache_diff_v)]
    idx_cache = _token_minor(cache_idx_k)
    xp = x_prompt
    xs = x_sample.reshape(1, bd * t_new, d)
    outs = {k: [] for k in ('p_bk', 'p_bv', 'p_ik', 'p_ck', 'p_cv', 'p_st', 's_bk', 's_bv', 's_ik', 's_ck', 's_cv', 's_st')}

    def pad_rows(a, n):
        return jnp.pad(a, ((0, 0), (0, n - a.shape[1]), (0, 0)))

    def tok_major(a_t, *feat):
        nf = len(feat)
        a = a_t.reshape((a_t.shape[0],) + feat + (a_t.shape[2],))
        return jnp.transpose(a, (0, nf + 1) + tuple(range(1, nf + 1)))

    def new_page(a_t):
        a = jnp.transpose(a_t[0].reshape(a_t.shape[1], bd, t_new), (1, 0, 2)).astype(BF16)
        return jnp.pad(a, ((0, 0), (0, 0), (0, PAGE_SIZE - t_new)))

    for l in range(depth):
        w_pack = _pack_w_in(w_in[l])
        norms = (dsa_qnorm_w[l], dsa_knorm_w[l], diff_qnorm_w[l], diff_knorm_w[l])

        o = _inproj(l, xp, pos_p, norm_w[l], w_pack, hgrn_lb_logits, *norms, tm=PROMPT_TILE)
        oa, st = _hgrn(o['hg'], jnp.zeros((bn, A_HEADS, A_DK, A_DK), F32), PROMPT_TILE, f"hgrn_prompt_l{l}")
        ob = _dsa_prompt(o, PROMPT_TILE)
        oc = _diff_prompt(l, o, diff_lambda[l], PROMPT_TILE)
        xp = _outproj(l, xp, oa, ob, oc, o['gate'], w_out[l], hgrn_onorm_w[l], diff_subln_w[l], min(512, s))
        outs['p_bk'].append(tok_major(o['bkt'], B_HEADS, B_HD))
        outs['p_bv'].append(tok_major(o['bvt32'], B_HEADS, B_HD))
        outs['p_ik'].append(tok_major(o['ikt'], IDX_D))
        outs['p_ck'].append(tok_major(o['ckt'], C_HEADS, 2, C_QK))
        outs['p_cv'].append(tok_major(o['cvt32'], C_HEADS, C_DV))
        outs['p_st'].append(st)

        o = _inproj(l, xs, pos_s, norm_w[l], w_pack, hgrn_lb_logits, *norms, tm=bd * t_new)
        per_seq = lambda a: a.reshape(bd, t_new, a.shape[-1])
        hg = pad_rows(per_seq(o['hg']), HGRN_CHUNK)
        oa, st = _hgrn(hg, state_hgrn[l], HGRN_CHUNK, f"hgrn_sample_l{l}")
        oa = oa[:, :t_new].reshape(1, bd * t_new, A_W)
        iq_rows = o['iqb'].reshape(bd, t_new * IDX_HEADS, IDX_D)
        w_rows = jnp.broadcast_to(jnp.swapaxes(o['iwt'][0], 0, 1).reshape(bd, t_new * IDX_HEADS, 1),
                                  (bd, t_new * IDX_HEADS, LANES))
        keys, thr = _sample_index(l, idx_cache, page_table, iq_rows, w_rows, new_page(o['ikt']), group,
                                  min(TOPK_MAX, (past + t_new) // 4))
        lane = jnp.arange(B_W)
        r8 = jnp.arange(8)
        d_mask = (lane[None, :] // B_HD == r8[:, None])
        c_mask = (lane[None, :] // C_QK == r8[:, None])
        qd = jnp.where(d_mask[None, None], per_seq(o['bqb'])[:, :, None, :], 0).reshape(bd, 8 * t_new, B_W)
        qc = jnp.where(c_mask[None, None], per_seq(o['cqb'])[:, :, None, :], 0).reshape(bd, 8 * t_new, C_W)
        new_kv = [new_page(o[k]) for k in ('bkt', 'bvt32', 'ckt', 'cvt32')]
        ob, oc = _sample_attn(l, caches, page_table, keys, thr, qd, qc, new_kv, diff_lambda[l], group)
        ob = ob[:, :t_new].reshape(1, bd * t_new, B_W)
        oc = oc[:, :t_new].reshape(1, bd * t_new, C_W)
        xs = _outproj(l, xs, oa, ob, oc, o['gate'], w_out[l], hgrn_onorm_w[l], diff_subln_w[l], bd * t_new)
        outs['s_bk'].append(tok_major(o['bkt'], B_HEADS, B_HD).reshape(bd, t_new, B_HEADS, B_HD))
        outs['s_bv'].append(tok_major(o['bvt32'], B_HEADS, B_HD).reshape(bd, t_new, B_HEADS, B_HD))
        outs['s_ik'].append(tok_major(o['ikt'], IDX_D).reshape(bd, t_new, IDX_D))
        outs['s_ck'].append(tok_major(o['ckt'], C_HEADS, 2, C_QK).reshape(bd, t_new, C_HEADS, 2, C_QK))
        outs['s_cv'].append(tok_major(o['cvt32'], C_HEADS, C_DV).reshape(bd, t_new, C_HEADS, C_DV))
        outs['s_st'].append(st)

    stk = {k: jnp.stack(v) for k, v in outs.items()}
    return (xp, xs.reshape(bd, t_new, d),
            stk['p_bk'], stk['p_bv'], stk['p_ik'], stk['p_ck'], stk['p_cv'], stk['p_st'],
            stk['s_bk'], stk['s_bv'], stk['s_ik'], stk['s_ck'], stk['s_cv'], stk['s_st'])
```

```python
import functools
import math

import numpy as np
import jax
import jax.numpy as jnp
from jax import lax
from jax.experimental import pallas as pl
from jax.experimental.pallas import tpu as pltpu

F32 = jnp.float32
BF16 = jnp.bfloat16
I32 = jnp.int32

D_MODEL = 1024
A_W, A_HEADS, A_DK = 256, 4, 64
B_W, B_HEADS, B_HD = 384, 6, 64
IDX_HEADS, IDX_D = 8, 64
C_W, C_HEADS, C_DV, C_QK = 384, 4, 96, 48
TOPK_MAX = 256
PAGE_SIZE = 128
ROPE_THETA = 500000.0
EPS = 1e-6
NEG_BIG = -1e30
LOG2E = math.log2(math.e)
HGRN_CHUNK = 64
LANES = 128
INT_MIN = -(2 ** 31)
VMEM_LIMIT = 56 * 1024 * 1024
PROMPT_TILE = 256
SAMPLE_PAGE_GROUP = 16
SAMPLE_INDEX_GROUP = 32
HGRN_SEQS_PER_STEP = 2

_OFF_A = 0
_OFF_BQ, _OFF_BK, _OFF_BV, _OFF_BG = 1024, 1408, 1792, 2176
_OFF_IQ, _OFF_IK, _OFF_IW = 2560, 3072, 3200
_OFF_CQ, _OFF_CK, _OFF_CV, _OFF_CG = 3328, 3712, 4096, 4480
_N_PACK = 4864


def _dot(a, b):
    return jnp.dot(a, b, preferred_element_type=F32)


def _dot_nt(a, b):
    return lax.dot_general(a, b, (((1,), (1,)), ((), ())), preferred_element_type=F32)


def _pack_w_in(w):
    i_k = w[:, 3072:3136]
    i_w = w[:, 3136:3144]
    pad = jnp.zeros((w.shape[0], LANES - IDX_HEADS), w.dtype)
    packed = jnp.concatenate([w[:, :3072], i_k, i_k, i_w, pad, w[:, 3144:]], axis=1).astype(BF16)
    wv_t = jnp.concatenate([w[:, 1792:2176], w[:, 3912:4296]], axis=1).T.astype(BF16)
    return packed, wv_t


def _rope_tables(pos, d, n_rep):
    rot = d // 4
    half = rot // 2
    inv = jnp.power(ROPE_THETA, -2.0 * jnp.arange(half, dtype=F32) / rot)
    ang = pos.astype(F32)[:, None] * inv[None, :]
    cos, sin = jnp.cos(ang), jnp.sin(ang)
    t = pos.shape[0]
    one = jnp.ones((t, d - rot), F32)
    zero = jnp.zeros((t, d - half), F32)
    c = jnp.concatenate([cos, cos, one], axis=1)
    sa = jnp.concatenate([-sin, zero], axis=1)
    sb = jnp.concatenate([jnp.zeros((t, half), F32), sin, jnp.zeros((t, d - rot), F32)], axis=1)
    return jnp.stack([jnp.tile(c, (1, n_rep)), jnp.tile(sa, (1, n_rep)), jnp.tile(sb, (1, n_rep))])


def _block_diag_ones(width, group):
    g = np.arange(width) // group
    return jnp.asarray((g[:, None] == g[None, :]).astype(np.float32), dtype=BF16)


def _pad_perm():
    p = np.zeros((C_W, C_HEADS * LANES), np.float32)
    for j in range(C_W):
        p[j, (j // C_DV) * LANES + (j % C_DV)] = 1.0
    return jnp.asarray(p, dtype=BF16)


def _rope(x, tab_ref, width, half):
    c = tab_ref[0, :, :width]
    sa = tab_ref[1, :, :width]
    sb = tab_ref[2, :, :width]
    return x * c + pltpu.roll(x, width - half, 1) * sa + pltpu.roll(x, half, 1) * sb


def _head_rms(x, m_ref, group, w_row):
    ms = _dot((x * x).astype(BF16), m_ref[...]) * (1.0 / group)
    return x * lax.rsqrt(ms + EPS) * w_row


def _inproj_kernel(layer, n_carried, x_ref, nw_ref, w_ref, wvt_ref, lbl_ref, t64_ref, t48_ref, m64_ref, m48_ref,
                   ppad_ref, bqn_ref, bkn_ref, cqn_ref, ckn_ref, *rest):
    (hg_ref, gate_ref, bkt_ref, bvt32_ref, ikt_ref, ckt_ref, cvt32_ref,
     bqb_ref, bkb_ref, bvt_ref, iqb_ref, ikd_ref, iwt_ref, cqb_ref, cqp_ref, ckp_ref, cvt_ref) = rest[n_carried:]
    xf = x_ref[0]
    y = xf * lax.rsqrt(jnp.mean(xf * xf, axis=-1, keepdims=True) + EPS)
    h = (y * nw_ref[...]).astype(BF16)

    def seg(a, b):
        return _dot(h, w_ref[:, a:b])

    lg = lbl_ref[...]
    e = jnp.exp(lg - jnp.max(lg, axis=0, keepdims=True))
    soft = e / jnp.sum(e, axis=0, keepdims=True)
    lb = jnp.zeros((1, A_W), F32)
    for i in range(1, layer + 1):
        lb = lb + soft[i:i + 1, :]
    fz = seg(_OFF_A + 256, _OFF_A + 512)
    en = jnp.exp(-jnp.abs(fz))
    r = 1.0 / (1.0 + en)
    pos_side = fz >= 0
    sig_p = jnp.where(pos_side, r, en * r)
    sig_n = jnp.where(pos_side, en * r, r)
    hg_ref[0, :, 0:256] = seg(_OFF_A, _OFF_A + 256)
    hg_ref[0, :, 256:512] = (1.0 - lb) * sig_n
    hg_ref[0, :, 512:768] = jnp.log(lb + (1.0 - lb) * sig_p)
    hg_ref[0, :, 768:1024] = seg(_OFF_A + 512, _OFF_A + 768)

    def silu(g):
        return g / (1.0 + jnp.exp(-g))

    gate_ref[0, :, 0:256] = silu(seg(_OFF_A + 768, _OFF_A + 1024))
    gate_ref[0, :, 256:640] = silu(seg(_OFF_BG, _OFF_BG + B_W))
    gate_ref[0, :, 640:1024] = silu(seg(_OFF_CG, _OFF_CG + C_W))

    bq = _rope(_head_rms(seg(_OFF_BQ, _OFF_BQ + B_W), m64_ref, B_HD, bqn_ref[...]), t64_ref, B_W, B_HD // 8)
    bqb_ref[0] = (bq * (B_HD ** -0.5 * LOG2E)).astype(BF16)
    bk = _rope(_head_rms(seg(_OFF_BK, _OFF_BK + B_W), m64_ref, B_HD, bkn_ref[...]), t64_ref, B_W, B_HD // 8)
    bkt_ref[...] = bk.T
    bkb_ref[0] = bk.astype(BF16)
    bvt = _dot_nt(wvt_ref[0:B_W, :], h)
    bvt32_ref[...] = bvt
    bvt_ref[0, 0] = bvt.astype(BF16)

    iq = _rope(seg(_OFF_IQ, _OFF_IQ + IDX_HEADS * IDX_D), t64_ref, IDX_HEADS * IDX_D, IDX_D // 8)
    iqb_ref[0] = (iq * (IDX_D ** -0.5)).astype(BF16)
    ikd = _rope(seg(_OFF_IK, _OFF_IK + LANES), t64_ref, LANES, IDX_D // 8)
    ikt_ref[...] = ikd.T[0:IDX_D, :]
    ikd_ref[0] = ikd.astype(BF16)
    iw = seg(_OFF_IW, _OFF_IW + LANES) * (IDX_HEADS ** -0.5)
    iwt_ref[0] = iw.T[0:IDX_HEADS, :]

    cq = _rope(_head_rms(seg(_OFF_CQ, _OFF_CQ + C_W), m48_ref, C_QK, cqn_ref[...]), t48_ref, C_W, C_QK // 8)
    cqb = (cq * (C_QK ** -0.5 * LOG2E)).astype(BF16)
    cqb_ref[0] = cqb
    cqp_ref[0] = _dot(cqb, ppad_ref[...]).astype(BF16)
    ck = _rope(_head_rms(seg(_OFF_CK, _OFF_CK + C_W), m48_ref, C_QK, ckn_ref[...]), t48_ref, C_W, C_QK // 8)
    ckt_ref[...] = ck.T
    ckp_ref[0] = _dot(ck.astype(BF16), ppad_ref[...]).astype(BF16)
    cvt = _dot_nt(wvt_ref[B_W:B_W + C_W, :], h)
    cvt32_ref[...] = cvt
    cvt_ref[0, 0] = cvt.astype(BF16)


_STACKED_OUTPUTS = ('bkt', 'bvt32', 'ikt', 'ckt', 'cvt32')


def _inproj(layer, x, pos, norm_w, w_packs, lb_logits, bqn, bkn, cqn, ckn, tm, depth=None, stacked=None):
    w_pack, wv_t = w_packs
    bn, t, d = x.shape
    nt = t // tm
    t64 = _rope_tables(pos, IDX_D, IDX_HEADS)
    t48 = _rope_tables(pos, C_QK, 2 * C_HEADS)
    m64 = _block_diag_ones(B_W, B_HD)
    m48 = _block_diag_ones(C_W, C_QK)
    ppad = _pad_perm()

    def tok(width, dtype):
        return jax.ShapeDtypeStruct((bn, t, width), dtype), pl.BlockSpec((1, tm, width), lambda i, b: (b, i, 0))

    def trf(rows):
        if depth is None:
            return (jax.ShapeDtypeStruct((bn, rows, t), F32), pl.BlockSpec((None, rows, tm), lambda i, b: (b, 0, i)))
        return (jax.ShapeDtypeStruct((depth, bn, rows, t), F32),
                pl.BlockSpec((None, None, rows, tm), lambda i, b: (layer, b, 0, i)))

    def tr(rows, dtype):
        return (jax.ShapeDtypeStruct((bn, nt, rows, tm), dtype),
                pl.BlockSpec((1, 1, rows, tm), lambda i, b: (b, i, 0, 0)))

    outs = dict(
        hg=tok(1024, F32), gate=tok(1024, F32), bkt=trf(B_W), bvt32=trf(B_W), ikt=trf(IDX_D), ckt=trf(C_W), cvt32=trf(C_W),
        bqb=tok(B_W, BF16), bkb=tok(B_W, BF16), bvt=tr(B_W, BF16), iqb=tok(IDX_HEADS * IDX_D, BF16),
        ikd=tok(LANES, BF16),
        iwt=(jax.ShapeDtypeStruct((bn, IDX_HEADS, t), F32), pl.BlockSpec((1, IDX_HEADS, tm), lambda i, b: (b, 0, i))),
        cqb=tok(C_W, BF16), cqp=tok(C_HEADS * LANES, BF16), ckp=tok(C_HEADS * LANES, BF16), cvt=tr(C_W, BF16),
    )
    names = list(outs)

    def const(shape):
        nd = len(shape)
        return pl.BlockSpec(shape, lambda i, b: (0,) * nd)

    in_specs = [
        pl.BlockSpec((1, tm, d), lambda i, b: (b, i, 0)),
        const((1, d)),
        const((d, _N_PACK)),
        const((B_W + C_W, d)),
        const(lb_logits.shape),
        pl.BlockSpec((3, tm, IDX_HEADS * IDX_D), lambda i, b: (0, i, 0)),
        pl.BlockSpec((3, tm, C_W), lambda i, b: (0, i, 0)),
        const((B_W, B_W)), const((C_W, C_W)), const((C_W, C_HEADS * LANES)),
        const((1, B_W)), const((1, B_W)), const((1, C_W)), const((1, C_W)),
    ]
    carried = [] if stacked is None else [stacked[n] for n in _STACKED_OUTPUTS]
    aliases = {len(in_specs) + j: names.index(n) for j, n in enumerate(_STACKED_OUTPUTS)} if carried else {}
    res = pl.pallas_call(
        functools.partial(_inproj_kernel, layer, len(carried)),
        grid=(nt, bn),
        in_specs=in_specs + [pl.BlockSpec(memory_space=pl.ANY)] * len(carried),
        out_specs=[outs[n][1] for n in names],
        out_shape=[outs[n][0] for n in names],
        input_output_aliases=aliases,
        compiler_params=pltpu.CompilerParams(dimension_semantics=("arbitrary", "arbitrary"),
                                             vmem_limit_bytes=VMEM_LIMIT),
        name=f"inproj_l{layer}_t{t}",
    )(x, norm_w.reshape(1, d), w_pack, wv_t, lb_logits, t64, t48, m64, m48, ppad,
      jnp.tile(bqn, B_HEADS).reshape(1, B_W), jnp.tile(bkn, B_HEADS).reshape(1, B_W),
      jnp.tile(cqn, 2 * C_HEADS).reshape(1, C_W), jnp.tile(ckn, 2 * C_HEADS).reshape(1, C_W), *carried)
    return dict(zip(names, res))


_HGRN_LEVELS = (1, 2, 4, 8, 16, 32)


def _hgrn_tables():
    c = HGRN_CHUNK
    t = np.arange(c)[:, None]
    u = np.arange(c)[None, :]
    mats = [(u <= t), (u > t)]
    masks = [(t == u)]
    for h in _HGRN_LEVELS:
        r = (t // (2 * h)) * (2 * h) + h - 1
        right = (t % (2 * h)) >= h
        mats.append(np.where(right, (u > r) & (u <= t), (u > t) & (u <= r)))
        masks.append((t // (2 * h) == u // (2 * h)) & right & ((u % (2 * h)) < h))
    tall = np.tile(np.concatenate(mats, axis=0).astype(np.float32), (1, 3))
    mk = np.stack([np.tile(m, (1, 2)) for m in masks]).astype(np.float32)
    return jnp.asarray(tall, dtype=BF16), jnp.asarray(mk)


def _pair_expand(a):
    lane = lax.broadcasted_iota(I32, a.shape, 1)
    zero = jnp.zeros_like(a)
    return jnp.concatenate([jnp.where(lane < A_DK, a, zero), jnp.where(lane >= A_DK, a, zero)], axis=0)


def _hgrn_kernel(nchunk, nseq, hg_ref, tall_ref, mk_ref, s0_ref, o_ref, sf_ref, st_ref):
    c = HGRN_CHUNK

    @pl.when(pl.program_id(1) == 0)
    def _():
        st_ref[...] = s0_ref[...]

    row = lax.broadcasted_iota(I32, (2 * A_DK, 2 * A_DK), 0)
    col = lax.broadcasted_iota(I32, (2 * A_DK, 2 * A_DK), 1)
    same_head = (row // A_DK) == (col // A_DK)
    zpad = jnp.zeros((c, 2 * A_DK), BF16)

    units = [(sq, p) for sq in range(nseq) for p in range(2)]
    n_lev = len(_HGRN_LEVELS)
    for ci in range(nchunk):
        rows = slice(ci * c, (ci + 1) * c)
        exs = []
        for sq in range(nseq):
            lf = hg_ref[sq, rows, 512:768]
            l1 = lf.astype(BF16)
            r1 = lf - l1.astype(F32)
            l2 = r1.astype(BF16)
            l3 = (r1 - l2.astype(F32)).astype(BF16)
            exs.append(_dot(tall_ref[...], jnp.concatenate([l1, l2, l3], axis=0)))
        exs = [jnp.exp(e) for e in exs]
        inter, level, upd, vexp, states = [], [], [], [], []
        for sq, p in units:
            ex = exs[sq]
            lanes = slice(p * 128, (p + 1) * 128)
            qp = hg_ref[sq, rows, lanes]
            kp = hg_ref[sq, rows, 256 + p * 128:256 + (p + 1) * 128]
            vp = hg_ref[sq, rows, 768 + p * 128:768 + (p + 1) * 128].astype(BF16)
            st = st_ref[sq, p]
            states.append(st)
            vexp.append(_pair_expand(vp))
            inter.append(_dot_nt((qp * ex[0:c, lanes]).astype(BF16), st.astype(BF16)))
            lv = [_dot_nt(qp.astype(BF16), _pair_expand(kp.astype(BF16)))]
            for li in range(n_lev):
                xl = ex[(2 + li) * c:(3 + li) * c, lanes]
                lv.append(_dot_nt((qp * xl).astype(BF16), _pair_expand((kp * xl).astype(BF16))))
            level.append(lv)
            kh = (kp * ex[c:2 * c, lanes]).astype(BF16)
            vt = jnp.concatenate([vp, zpad], axis=0).astype(F32).T.astype(BF16)
            upd.append(_dot(vt, jnp.concatenate([kh, zpad], axis=0)))
        scores = []
        for lv in level:
            sc = mk_ref[0] * lv[0]
            for li in range(n_lev):
                sc = sc + mk_ref[1 + li] * lv[1 + li]
            scores.append(sc.astype(BF16))
        intra = [_dot(sc, ve) for sc, ve in zip(scores, vexp)]
        for u, (sq, p) in enumerate(units):
            lanes = slice(p * 128, (p + 1) * 128)
            o_ref[sq, rows, lanes] = inter[u] + intra[u]
            decay = exs[sq][c - 1:c, lanes]
            st_ref[sq, p] = jnp.where(same_head, states[u] * decay + upd[u], 0.0)
    sf_ref[...] = st_ref[...]


def _state_to_pairs(s):
    bn = s.shape[0]
    st = jnp.swapaxes(s, -1, -2).reshape(bn, 2, 2, A_DK, A_DK)
    z = jnp.zeros((bn, 2, A_DK, A_DK), s.dtype)
    top = jnp.concatenate([st[:, :, 0], z], axis=-1)
    bot = jnp.concatenate([z, st[:, :, 1]], axis=-1)
    return jnp.concatenate([top, bot], axis=-2)


def _pairs_to_state(sp):
    bn = sp.shape[0]
    h0 = sp[:, :, :A_DK, :A_DK]
    h1 = sp[:, :, A_DK:, A_DK:]
    return jnp.swapaxes(jnp.stack([h0, h1], axis=2).reshape(bn, A_HEADS, A_DK, A_DK), -1, -2)


def _hgrn(hg, s0, tb, name):
    bn, t, _ = hg.shape
    nseq = HGRN_SEQS_PER_STEP
    assert bn % nseq == 0
    tall, mk = _hgrn_tables()
    o, sf = pl.pallas_call(
        functools.partial(_hgrn_kernel, tb // HGRN_CHUNK, nseq),
        grid=(bn // nseq, t // tb),
        in_specs=[pl.BlockSpec((nseq, tb, 1024), lambda b, j: (b, j, 0)),
                  pl.BlockSpec(tall.shape, lambda b, j: (0, 0)),
                  pl.BlockSpec(mk.shape, lambda b, j: (0, 0, 0)),
                  pl.BlockSpec((nseq, 2, 128, 128), lambda b, j: (b, 0, 0, 0))],
        out_specs=[pl.BlockSpec((nseq, tb, A_W), lambda b, j: (b, j, 0)),
                   pl.BlockSpec((nseq, 2, 128, 128), lambda b, j: (b, 0, 0, 0))],
        out_shape=[jax.ShapeDtypeStruct((bn, t, A_W), F32), jax.ShapeDtypeStruct((bn, 2, 128, 128), F32)],
        scratch_shapes=[pltpu.VMEM((nseq, 2, 128, 128), F32)],
        compiler_params=pltpu.CompilerParams(dimension_semantics=("arbitrary", "arbitrary"),
                                             vmem_limit_bytes=VMEM_LIMIT),
        name=name,
    )(hg, tall, mk, _state_to_pairs(s0))
    return o, _pairs_to_state(sf)


DSA_TQ = 256


def _key_of(score):
    bits = lax.bitcast_convert_type(score, I32)
    key = bits ^ ((bits >> 31) & 0x7FFFFFFF)
    return jnp.where(score == 0.0, 0, key)


def _count_rows(pred_fn, nk, ch, width):
    def body(kc, acc):
        hit = jnp.where(pred_fn(kc), 1, 0)
        return acc + jnp.sum(hit.reshape(ch // 8, 8, width), axis=0)

    acc = lax.fori_loop(0, nk, body, jnp.zeros((8, width), I32))
    return jnp.sum(acc, axis=0, keepdims=True)


def _select_topk(keys_ref, nk, ch, width, n_keep, pos_bits):
    def chunk(kc):
        return keys_ref[pl.ds(kc * ch, ch), :]

    def kpos(kc):
        return kc * ch + lax.broadcasted_iota(I32, (ch, width), 0)

    def bit_step(i, tu):
        cand = tu | jnp.left_shift(jnp.int32(1), 31 - i)
        cnt = _count_rows(lambda kc: chunk(kc) >= (cand ^ INT_MIN), nk, ch, width)
        return jnp.where(cnt >= n_keep, cand, tu)

    thr = lax.fori_loop(0, 32, bit_step, jnp.zeros((1, width), I32)) ^ INT_MIN
    cnt_ge = _count_rows(lambda kc: chunk(kc) >= thr, nk, ch, width)
    excess = jnp.where(thr > INT_MIN, cnt_ge - n_keep, 0)

    @pl.when(jnp.max(excess) > 0)
    def _():
        need = n_keep - _count_rows(lambda kc: chunk(kc) > thr, nk, ch, width)

        def pos_step(i, c):
            cand = c | jnp.left_shift(jnp.int32(1), pos_bits - 1 - i)
            before = _count_rows(lambda kc: (chunk(kc) == thr) & (kpos(kc) < cand), nk, ch, width)
            return jnp.where(before <= need - 1, cand, c)

        cut = lax.fori_loop(0, pos_bits, pos_step, jnp.zeros((1, width), I32))

        def drop(kc, carry):
            k = chunk(kc)
            late_tie = (k == thr) & (kpos(kc) > cut) & (excess > 0)
            keys_ref[pl.ds(kc * ch, ch), :] = jnp.where(late_tie, INT_MIN, k)
            return carry

        lax.fori_loop(0, nk, drop, 0)

    return jnp.maximum(thr, INT_MIN + 1)


def _lane_halves(x):
    lane = lax.broadcasted_iota(I32, x.shape, 1)
    zero = jnp.zeros_like(x)
    return jnp.concatenate([jnp.where(lane < B_HD, x, zero), jnp.where(lane >= B_HD, x, zero)], axis=0)


def _softmax_updates_t(scores, values_t, m_ref, l_ref, acc_ref):
    alphas, probs = [], []
    for j, s in enumerate(scores):
        m_old = m_ref[j, 0:1, :]
        m_new = jnp.maximum(m_old, jnp.max(s, axis=0, keepdims=True))
        alpha = jnp.exp2(m_old - m_new)
        e = jnp.exp2(s - m_new)
        l_ref[j, 0:1, :] = alpha * l_ref[j, 0:1, :] + jnp.sum(e, axis=0, keepdims=True)
        m_ref[j, 0:1, :] = m_new
        alphas.append(alpha)
        probs.append(e.astype(BF16))
    updates = [_dot(v_t, p) for v_t, p in zip(values_t, probs)]
    for j, (alpha, upd) in enumerate(zip(alphas, updates)):
        acc_ref[j] = alpha * acc_ref[j] + upd


def _dsa_prompt_kernel(n_keep, ch, pos_bits, iqb_ref, iwt_ref, ikd_ref, bqb_ref, bkb_ref, bvt_ref, o_ref,
                       keys_ref, m_ref, l_ref, acc_ref):
    tq = DSA_TQ
    qb = pl.program_id(1)
    nk = (qb * tq + tq + ch - 1) // ch
    qpos = qb * tq + lax.broadcasted_iota(I32, (1, tq), 1)

    iq = iqb_ref[0]
    iq_rows = jnp.concatenate([_lane_halves(iq[:, j * LANES:(j + 1) * LANES]) for j in range(IDX_HEADS // 2)], axis=0)
    wt = iwt_ref[0]

    def index_chunk(kc, carry):
        hc = ch // 2
        raw = [_dot_nt(ikd_ref[0, pl.ds(kc * ch + i * hc, hc), :], iq_rows) for i in range(2)]
        for i, s in enumerate(raw):
            acc = jnp.zeros((hc, tq), F32)
            for h in range(IDX_HEADS):
                acc = acc + wt[h:h + 1, :] * jnp.maximum(s[:, h * tq:(h + 1) * tq], 0.0)
            kp = kc * ch + i * hc + lax.broadcasted_iota(I32, (hc, tq), 0)
            keys_ref[pl.ds(kc * ch + i * hc, hc), :] = jnp.where(kp <= qpos, _key_of(acc), INT_MIN)
        return carry

    lax.fori_loop(0, nk, index_chunk, 0)

    @pl.when(nk % 2 == 1)
    def _():
        keys_ref[pl.ds(nk * ch, ch), :] = jnp.full((ch, tq), INT_MIN, I32)

    thr = _select_topk(keys_ref, (nk + 1) // 2, 2 * ch, tq, n_keep, pos_bits)

    n_pairs = B_HEADS // 2
    bq = bqb_ref[0]
    q_rows = [_lane_halves(bq[:, p * LANES:(p + 1) * LANES]) for p in range(n_pairs)]
    m_ref[...] = jnp.full(m_ref.shape, NEG_BIG, F32)
    l_ref[...] = jnp.zeros(l_ref.shape, F32)
    acc_ref[...] = jnp.zeros(acc_ref.shape, F32)

    def attend(kc, carry):
        sel = keys_ref[pl.ds(kc * ch, ch), :] >= thr
        raw = [_dot_nt(bkb_ref[0, pl.ds(kc * ch, ch), p * LANES:(p + 1) * LANES], q_rows[p])
               for p in range(n_pairs)]
        scores = [jnp.concatenate([jnp.where(sel, s[:, :tq], NEG_BIG), jnp.where(sel, s[:, tq:], NEG_BIG)], axis=1)
                  for s in raw]
        values_t = [bvt_ref[0, kc, p * LANES:(p + 1) * LANES, :] for p in range(n_pairs)]
        _softmax_updates_t(scores, values_t, m_ref, l_ref, acc_ref)
        return carry

    lax.fori_loop(0, nk, attend, 0)
    outs = []
    for p in range(n_pairs):
        inv = 1.0 / l_ref[p, 0:1, :]
        outs.append(acc_ref[p, 0:B_HD, 0:tq] * inv[:, 0:tq])
        outs.append(acc_ref[p, B_HD:, tq:] * inv[:, tq:])
    o_ref[0] = jnp.concatenate(outs, axis=0).T


def _dsa_prompt(o, ch):
    bn, s, _ = o['bkb'].shape
    n_keep = min(TOPK_MAX, s // 4)
    nb = s // DSA_TQ
    assert (s // ch) % 2 == 0, "the threshold search walks key chunks in pairs"
    n_pairs = B_HEADS // 2
    return pl.pallas_call(
        functools.partial(_dsa_prompt_kernel, n_keep, ch, s.bit_length()),
        grid=(bn, nb),
        in_specs=[pl.BlockSpec((1, DSA_TQ, IDX_HEADS * IDX_D), lambda b, q: (b, q, 0)),
                  pl.BlockSpec((1, IDX_HEADS, DSA_TQ), lambda b, q: (b, 0, q)),
                  pl.BlockSpec((1, s, LANES), lambda b, q: (b, 0, 0)),
                  pl.BlockSpec((1, DSA_TQ, B_W), lambda b, q: (b, q, 0)),
                  pl.BlockSpec((1, s, B_W), lambda b, q: (b, 0, 0)),
                  pl.BlockSpec((1, s // ch, B_W, ch), lambda b, q: (b, 0, 0, 0))],
        out_specs=pl.BlockSpec((1, DSA_TQ, B_W), lambda b, q: (b, q, 0)),
        out_shape=jax.ShapeDtypeStruct((bn, s, B_W), F32),
        scratch_shapes=[pltpu.VMEM((s, DSA_TQ), I32), pltpu.VMEM((n_pairs, 8, 2 * DSA_TQ), F32),
                        pltpu.VMEM((n_pairs, 8, 2 * DSA_TQ), F32), pltpu.VMEM((n_pairs, LANES, 2 * DSA_TQ), F32)],
        compiler_params=pltpu.CompilerParams(dimension_semantics=("arbitrary", "arbitrary"),
                                             vmem_limit_bytes=VMEM_LIMIT),
        name="dsa_prompt",
    )(o['iqb'], o['iwt'], o['ikd'], o['bqb'], o['bkb'], o['bvt'])


def _diff_lambda(dl, lam_init):
    a = jnp.sum(dl[0:1, :] * dl[1:2, :], axis=1, keepdims=True)
    b = jnp.sum(dl[2:3, :] * dl[3:4, :], axis=1, keepdims=True)
    return jnp.exp(a) - jnp.exp(b) + lam_init


def _diff_prompt_kernel(lam_init, ch, dl_ref, cqp_ref, ckp_ref, cvt_ref, o_ref, m_ref, l_ref, acc_ref):
    tq = DSA_TQ
    qb = pl.program_id(1)
    nk = (qb * tq + tq + ch - 1) // ch
    n_full = (qb * tq) // ch
    qpos = qb * tq + lax.broadcasted_iota(I32, (1, tq), 1)
    lam = _diff_lambda(dl_ref[...], lam_init)
    cq = cqp_ref[0]
    lane = lax.broadcasted_iota(I32, (tq, LANES), 1)
    q_rows = []
    for h in range(C_HEADS):
        qh = cq[:, h * LANES:(h + 1) * LANES]
        zero = jnp.zeros_like(qh)
        q_rows.append(jnp.concatenate([jnp.where(lane < C_QK, qh, zero),
                                       jnp.where((lane >= C_QK) & (lane < 2 * C_QK), qh, zero)], axis=0))
    m_ref[...] = jnp.full(m_ref.shape, NEG_BIG, F32)
    l_ref[...] = jnp.zeros(l_ref.shape, F32)
    acc_ref[...] = jnp.zeros(acc_ref.shape, F32)

    def attend(kc, masked):
        scores = [_dot_nt(ckp_ref[0, pl.ds(kc * ch, ch), h * LANES:(h + 1) * LANES], q_rows[h])
                  for h in range(C_HEADS)]
        if masked:
            vis = (kc * ch + lax.broadcasted_iota(I32, (ch, tq), 0)) <= qpos
            scores = [jnp.concatenate([jnp.where(vis, s[:, :tq], NEG_BIG), jnp.where(vis, s[:, tq:], NEG_BIG)],
                                      axis=1) for s in scores]
        values_t = [cvt_ref[0, kc, h * C_DV:(h + 1) * C_DV, :] for h in range(C_HEADS)]
        _softmax_updates_t(scores, values_t, m_ref, l_ref, acc_ref)

    def full_chunk(kc, carry):
        attend(kc, False)
        return carry

    def diag_chunk(kc, carry):
        attend(kc, True)
        return carry

    lax.fori_loop(0, n_full, full_chunk, 0)
    lax.fori_loop(n_full, nk, diag_chunk, 0)
    outs = []
    for h in range(C_HEADS):
        inv = 1.0 / l_ref[h, 0:1, :]
        acc = acc_ref[h]
        outs.append(acc[:, :tq] * inv[:, :tq] - lam * (acc[:, tq:] * inv[:, tq:]))
    o_ref[0] = jnp.concatenate(outs, axis=0).T


def _diff_prompt(layer, o, dl, ch):
    bn, s, _ = o['ckp'].shape
    lam_init = 0.8 - 0.6 * math.exp(-0.3 * layer)
    return pl.pallas_call(
        functools.partial(_diff_prompt_kernel, lam_init, ch),
        grid=(bn, s // DSA_TQ),
        in_specs=[pl.BlockSpec((4, C_QK), lambda b, q: (0, 0)),
                  pl.BlockSpec((1, DSA_TQ, C_HEADS * LANES), lambda b, q: (b, q, 0)),
                  pl.BlockSpec((1, s, C_HEADS * LANES), lambda b, q: (b, 0, 0)),
                  pl.BlockSpec((1, s // ch, C_W, ch), lambda b, q: (b, 0, 0, 0))],
        out_specs=pl.BlockSpec((1, DSA_TQ, C_W), lambda b, q: (b, q, 0)),
        out_shape=jax.ShapeDtypeStruct((bn, s, C_W), F32),
        scratch_shapes=[pltpu.VMEM((C_HEADS, 8, 2 * DSA_TQ), F32), pltpu.VMEM((C_HEADS, 8, 2 * DSA_TQ), F32),
                        pltpu.VMEM((C_HEADS, C_DV, 2 * DSA_TQ), F32)],
        compiler_params=pltpu.CompilerParams(dimension_semantics=("arbitrary", "arbitrary"),
                                             vmem_limit_bytes=VMEM_LIMIT),
        name="diff_prompt",
    )(dl, o['cqp'], o['ckp'], o['cvt'])


def _token_minor(cache):
    nd = cache.ndim
    c = jnp.transpose(cache, (0, 1) + tuple(range(3, nd)) + (2,))
    return c.reshape(c.shape[0], c.shape[1], -1, c.shape[-1])


def _page_specs(layer, n, group, width):
    def spec(j):
        return pl.BlockSpec((None, None, width, PAGE_SIZE), lambda b, g, pt: (layer, pt[b, g * group + j], 0, 0))

    return [spec(j) for _ in range(n) for j in range(group)]


def _rows_of_queries(k4, t_new):
    return jnp.concatenate([jnp.broadcast_to(k4[t:t + 1, :], (8, k4.shape[1])) for t in range(t_new)], axis=0)


def _select_topk_rows(keys_ref, lp, n_keep, pos_bits):
    cw = 16 * LANES

    def count(pred):
        acc = jnp.zeros((8, LANES), I32)
        for c0 in range(0, lp, cw):
            w = min(cw, lp - c0)
            hit = jnp.where(pred(keys_ref[0, :, c0:c0 + w], c0, w), 1, 0)
            for j in range(w // LANES):
                acc = acc + hit[:, j * LANES:(j + 1) * LANES]
        return jnp.sum(acc.astype(F32), axis=1, keepdims=True).astype(I32)

    def kpos(c0, w):
        return c0 + lax.broadcasted_iota(I32, (8, w), 1)

    def bit_step(i, tu):
        cand = tu | jnp.left_shift(jnp.int32(1), 31 - i)
        cnt = count(lambda k, c0, w: k >= (cand ^ INT_MIN))
        return jnp.where(cnt >= n_keep, cand, tu)

    thr = lax.fori_loop(0, 32, bit_step, jnp.zeros((8, 1), I32)) ^ INT_MIN
    excess = jnp.where(thr > INT_MIN, count(lambda k, c0, w: k >= thr) - n_keep, 0)

    @pl.when(jnp.max(excess) > 0)
    def _():
        need = n_keep - count(lambda k, c0, w: k > thr)

        def pos_step(i, c):
            cand = c | jnp.left_shift(jnp.int32(1), pos_bits - 1 - i)
            before = count(lambda k, c0, w: (k == thr) & (kpos(c0, w) < cand))
            return jnp.where(before <= need - 1, cand, c)

        cut = lax.fori_loop(0, pos_bits, pos_step, jnp.zeros((8, 1), I32))
        for c0 in range(0, lp, cw):
            w = min(cw, lp - c0)
            k = keys_ref[0, :, c0:c0 + w]
            late_tie = (k == thr) & (kpos(c0, w) > cut) & (excess > 0)
            keys_ref[0, :, c0:c0 + w] = jnp.where(late_tie, INT_MIN, k)

    return jnp.maximum(thr, INT_MIN + 1)


def _sample_index_kernel(group, n_groups, t_new, n_keep, pos_bits, pt_ref, *refs):
    pages = refs[:group]
    iq_ref, w_ref, iknew_ref, keys_ref, thr_ref = refs[group:]
    g = pl.program_id(1)
    gk = group * PAGE_SIZE
    past = n_groups * gk
    iq = iq_ref[0]
    wcol = w_ref[0][:, 0:1]

    def scores(kmat_t):
        s = jnp.maximum(_dot(iq, kmat_t), 0.0) * wcol
        return jnp.concatenate([jnp.sum(s[8 * t:8 * t + 8, :], axis=0, keepdims=True) for t in range(t_new)], axis=0)

    kcat = jnp.concatenate([p[...].astype(BF16) for p in pages], axis=1)
    key = _key_of(scores(kcat))
    key8 = jnp.concatenate([key] * (8 // t_new), axis=0)
    for gg in range(n_groups):
        @pl.when(g == gg)
        def _(gg=gg):
            keys_ref[0, :, gg * gk:(gg + 1) * gk] = key8

    @pl.when(g == n_groups - 1)
    def _():
        kn = _key_of(scores(iknew_ref[0]))
        row = lax.broadcasted_iota(I32, kn.shape, 0)
        lane = lax.broadcasted_iota(I32, kn.shape, 1)
        kn = jnp.where((lane <= row) & (lane < t_new), kn, INT_MIN)
        keys_ref[0, :, past:past + LANES] = jnp.concatenate([kn] * (8 // t_new), axis=0)
        thr = _select_topk_rows(keys_ref, past + LANES, n_keep, pos_bits)
        thr_ref[0] = jnp.broadcast_to(thr, (8, LANES))


def _sample_index(layer, cache_idx_k, page_table, iq_rows, w_rows, iknew, group, n_keep):
    bd, n_pages = page_table.shape
    n_groups = n_pages // group
    t_new = iq_rows.shape[1] // 8
    past = n_pages * PAGE_SIZE
    lp = past + LANES
    grid_spec = pltpu.PrefetchScalarGridSpec(
        num_scalar_prefetch=1,
        grid=(bd, n_groups),
        in_specs=_page_specs(layer, 1, group, IDX_D) + [
            pl.BlockSpec((1, 8 * t_new, IDX_D), lambda b, g, pt: (b, 0, 0)),
            pl.BlockSpec((1, 8 * t_new, LANES), lambda b, g, pt: (b, 0, 0)),
            pl.BlockSpec((1, IDX_D, PAGE_SIZE), lambda b, g, pt: (b, 0, 0))],
        out_specs=[pl.BlockSpec((1, 8, lp), lambda b, g, pt: (b, 0, 0)),
                   pl.BlockSpec((1, 8, LANES), lambda b, g, pt: (b, 0, 0))],
    )
    return pl.pallas_call(
        functools.partial(_sample_index_kernel, group, n_groups, t_new, n_keep, lp.bit_length()),
        grid_spec=grid_spec,
        out_shape=[jax.ShapeDtypeStruct((bd, 8, lp), I32), jax.ShapeDtypeStruct((bd, 8, LANES), I32)],
        compiler_params=pltpu.CompilerParams(dimension_semantics=("arbitrary", "arbitrary"),
                                             vmem_limit_bytes=VMEM_LIMIT),
        name=f"sample_index_l{layer}",
    )(page_table, *([cache_idx_k] * group), iq_rows, w_rows, iknew)


def _softmax_step(s, valid, v_t, m_ref, l_ref, acc_ref):
    if valid is not None:
        s = jnp.where(valid, s, NEG_BIG)
    m_old = m_ref[:, 0:1]
    m_new = jnp.maximum(m_old, jnp.max(s, axis=1, keepdims=True))
    alpha = jnp.exp2(m_old - m_new)
    e = jnp.exp2(s - m_new)
    if valid is not None:
        e = jnp.where(valid, e, 0.0)
    l_new = alpha * l_ref[:, 0:1] + jnp.sum(e, axis=1, keepdims=True)
    acc_ref[...] = alpha * acc_ref[...] + _dot_nt(e.astype(BF16), v_t)
    m_ref[...] = jnp.broadcast_to(m_new, m_ref.shape)
    l_ref[...] = jnp.broadcast_to(l_new, l_ref.shape)


def _sample_attn_kernel(group, n_groups, t_new, lam_init, pt_ref, *refs):
    n = group
    dk_pages, dv_pages, ck_pages, cv_pages = refs[0:n], refs[n:2 * n], refs[2 * n:3 * n], refs[3 * n:4 * n]
    (keys_ref, keysn_ref, thr_ref, qd_ref, qc_ref, dkn_ref, dvn_ref, ckn_ref, cvn_ref, dl_ref,
     ob_ref, oc_ref, md_ref, ld_ref, ad_ref, mc_ref, lc_ref, ac_ref) = refs[4 * n:]
    g = pl.program_id(1)
    rows = 8 * t_new

    @pl.when(g == 0)
    def _():
        for m_ref, l_ref, a_ref in ((md_ref, ld_ref, ad_ref), (mc_ref, lc_ref, ac_ref)):
            m_ref[...] = jnp.full(m_ref.shape, NEG_BIG, F32)
            l_ref[...] = jnp.zeros(l_ref.shape, F32)
            a_ref[...] = jnp.zeros(a_ref.shape, F32)

    def cat(pages):
        return jnp.concatenate([p[...].astype(BF16) for p in pages], axis=1)

    thr = _rows_of_queries(thr_ref[0], t_new)[:, 0:1]
    qd = qd_ref[0]
    qc = qc_ref[0]
    sel = _rows_of_queries(keys_ref[0], t_new) >= thr
    _softmax_step(_dot(qd, cat(dk_pages)), sel, cat(dv_pages), md_ref, ld_ref, ad_ref)
    _softmax_step(_dot(qc, cat(ck_pages)), None, cat(cv_pages), mc_ref, lc_ref, ac_ref)

    @pl.when(g == n_groups - 1)
    def _():
        row = lax.broadcasted_iota(I32, (rows, LANES), 0)
        lane = lax.broadcasted_iota(I32, (rows, LANES), 1)
        causal = (lane <= row // 8) & (lane < t_new)
        seln = causal & (_rows_of_queries(keysn_ref[0], t_new) >= thr)
        _softmax_step(_dot(qd, dkn_ref[0]), seln, dvn_ref[0], md_ref, ld_ref, ad_ref)
        _softmax_step(_dot(qc, ckn_ref[0]), causal, cvn_ref[0], mc_ref, lc_ref, ac_ref)

        r = lax.broadcasted_iota(I32, (rows, B_W), 0) % 8
        ln = lax.broadcasted_iota(I32, (rows, B_W), 1)
        od = jnp.where(ln // B_HD == r, ad_ref[...] / ld_ref[:, 0:1], 0.0)
        lam = _diff_lambda(dl_ref[...], lam_init)
        coef = jnp.where(r % 2 == 0, 1.0, -lam)
        oc = jnp.where(ln // C_DV == r // 2, coef * (ac_ref[...] / lc_ref[:, 0:1]), 0.0)
        pad = jnp.zeros((8 - t_new, B_W), F32)
        ob_ref[0] = jnp.concatenate([jnp.sum(od[8 * t:8 * t + 8], axis=0, keepdims=True) for t in range(t_new)] + [pad],
                                    axis=0)
        oc_ref[0] = jnp.concatenate([jnp.sum(oc[8 * t:8 * t + 8], axis=0, keepdims=True) for t in range(t_new)] + [pad],
                                    axis=0)


def _sample_attn(layer, caches, page_table, keys, thr, qd, qc, new_kv, dl, group):
    bd, n_pages = page_table.shape
    n_groups = n_pages // group
    rows = qd.shape[1]
    t_new = rows // 8
    gk = group * PAGE_SIZE
    past = n_pages * PAGE_SIZE
    lam_init = 0.8 - 0.6 * math.exp(-0.3 * layer)

    def per_seq(shape):
        nd = len(shape)
        return pl.BlockSpec((1,) + shape, lambda b, g, pt: (b,) + (0,) * nd)

    grid_spec = pltpu.PrefetchScalarGridSpec(
        num_scalar_prefetch=1,
        grid=(bd, n_groups),
        in_specs=_page_specs(layer, 4, group, B_W) + [
            pl.BlockSpec((1, 8, gk), lambda b, g, pt: (b, 0, g)),
            pl.BlockSpec((1, 8, LANES), lambda b, g, pt: (b, 0, past // LANES)),
            per_seq((8, LANES)), per_seq((rows, B_W)), per_seq((rows, C_W)),
            per_seq((B_W, PAGE_SIZE)), per_seq((B_W, PAGE_SIZE)), per_seq((C_W, PAGE_SIZE)), per_seq((C_W, PAGE_SIZE)),
            pl.BlockSpec((4, C_QK), lambda b, g, pt: (0, 0))],
        out_specs=[per_seq((8, B_W)), per_seq((8, C_W))],
        scratch_shapes=[pltpu.VMEM((rows, LANES), F32), pltpu.VMEM((rows, LANES), F32), pltpu.VMEM((rows, B_W), F32),
                        pltpu.VMEM((rows, LANES), F32), pltpu.VMEM((rows, LANES), F32), pltpu.VMEM((rows, C_W), F32)],
    )
    pages = [c for c in caches for _ in range(group)]
    return pl.pallas_call(
        functools.partial(_sample_attn_kernel, group, n_groups, t_new, lam_init),
        grid_spec=grid_spec,
        out_shape=[jax.ShapeDtypeStruct((bd, 8, B_W), F32), jax.ShapeDtypeStruct((bd, 8, C_W), F32)],
        compiler_params=pltpu.CompilerParams(dimension_semantics=("arbitrary", "arbitrary"),
                                             vmem_limit_bytes=VMEM_LIMIT),
        name=f"sample_attn_l{layer}",
    )(page_table, *pages, keys, keys, thr, qd, qc, *new_kv, dl)


def _outproj_kernel(c_scale, x_ref, oa_ref, ob_ref, oc_ref, gate_ref, w_ref, ma_ref, mc_ref, na_ref, nc_ref, y_ref):
    ga = _head_rms(oa_ref[0], ma_ref, A_DK, na_ref[...]) * gate_ref[0, :, 0:A_W]
    gb = ob_ref[0] * gate_ref[0, :, A_W:A_W + B_W]
    gc = _head_rms(oc_ref[0], mc_ref, C_DV, nc_ref[...]) * c_scale * gate_ref[0, :, A_W + B_W:]
    y = x_ref[0] + _dot(ga.astype(BF16), w_ref[0:A_W, :])
    y = y + _dot(gb.astype(BF16), w_ref[A_W:A_W + B_W, :])
    y_ref[0] = y + _dot(gc.astype(BF16), w_ref[A_W + B_W:, :])


def _outproj(layer, x, oa, ob, oc, gate, w_out, onorm_w, subln_w, tm):
    bn, t, d = x.shape
    lam_init = 0.8 - 0.6 * math.exp(-0.3 * layer)

    def tok(width):
        return pl.BlockSpec((1, tm, width), lambda b, i: (b, i, 0))

    def const(shape):
        return pl.BlockSpec(shape, lambda b, i: (0, 0))

    return pl.pallas_call(
        functools.partial(_outproj_kernel, 1.0 - lam_init),
        grid=(bn, t // tm),
        in_specs=[tok(d), tok(A_W), tok(B_W), tok(C_W), tok(d), const((d, d)), const((A_W, A_W)), const((C_W, C_W)),
                  const((1, A_W)), const((1, C_W))],
        out_specs=tok(d),
        out_shape=jax.ShapeDtypeStruct((bn, t, d), F32),
        compiler_params=pltpu.CompilerParams(dimension_semantics=("arbitrary", "arbitrary"),
                                             vmem_limit_bytes=VMEM_LIMIT),
        name=f"outproj_l{layer}_t{t}",
    )(x, oa, ob, oc, gate, w_out.astype(BF16), _block_diag_ones(A_W, A_DK), _block_diag_ones(C_W, C_DV),
      jnp.tile(onorm_w, A_HEADS).reshape(1, A_W), jnp.tile(subln_w, C_HEADS).reshape(1, C_W))


def kernel(x_prompt, x_sample, cache_dsa_k, cache_dsa_v, cache_idx_k, cache_diff_k, cache_diff_v, state_hgrn,
           page_table, norm_w, w_in, w_out, hgrn_lb_logits, hgrn_onorm_w, dsa_qnorm_w, dsa_knorm_w,
           diff_qnorm_w, diff_knorm_w, diff_lambda, diff_subln_w):
    depth = norm_w.shape[0]
    bn, s, d = x_prompt.shape
    bd, t_new, _ = x_sample.shape
    n_pool = cache_dsa_k.shape[1]
    n_pages = page_table.shape[1]
    past = n_pages * PAGE_SIZE
    group = math.gcd(SAMPLE_PAGE_GROUP, n_pages)
    pos_p = jnp.arange(s, dtype=I32)
    pos_s = past + jnp.arange(bd * t_new, dtype=I32) % t_new
    caches = [_token_minor(c) for c in (cache_dsa_k, cache_dsa_v, cache_diff_k, cache_diff_v)]
    idx_cache = _token_minor(cache_idx_k)
    xp = x_prompt
    xs = x_sample.reshape(1, bd * t_new, d)
    outs = {k: [] for k in ('p_bk', 'p_bv', 'p_ik', 'p_ck', 'p_cv', 'p_st', 's_bk', 's_bv', 's_ik', 's_ck', 's_cv', 's_st')}

    def pad_rows(a, n):
        return jnp.pad(a, ((0, 0), (0, n - a.shape[1]), (0, 0)))

    def tok_major(a_t, *feat):
        nf = len(feat)
        a = a_t.reshape((a_t.shape[0],) + feat + (a_t.shape[2],))
        return jnp.transpose(a, (0, nf + 1) + tuple(range(1, nf + 1)))

    def new_page(a_t):
        a = jnp.transpose(a_t[0].reshape(a_t.shape[1], bd, t_new), (1, 0, 2)).astype(BF16)
        return jnp.pad(a, ((0, 0), (0, 0), (0, PAGE_SIZE - t_new)))

    prompt_kv = None
    for l in range(depth):
        w_pack = _pack_w_in(w_in[l])
        norms = (dsa_qnorm_w[l], dsa_knorm_w[l], diff_qnorm_w[l], diff_knorm_w[l])

        o = _inproj(l, xp, pos_p, norm_w[l], w_pack, hgrn_lb_logits, *norms, tm=PROMPT_TILE, depth=depth,
                    stacked=prompt_kv)
        prompt_kv = {n: o[n] for n in _STACKED_OUTPUTS}
        oa, st = _hgrn(o['hg'], jnp.zeros((bn, A_HEADS, A_DK, A_DK), F32), PROMPT_TILE, f"hgrn_prompt_l{l}")
        ob = _dsa_prompt(o, PROMPT_TILE)
        oc = _diff_prompt(l, o, diff_lambda[l], PROMPT_TILE)
        xp = _outproj(l, xp, oa, ob, oc, o['gate'], w_out[l], hgrn_onorm_w[l], diff_subln_w[l], min(512, s))
        outs['p_st'].append(st)

        o = _inproj(l, xs, pos_s, norm_w[l], w_pack, hgrn_lb_logits, *norms, tm=bd * t_new)
        per_seq = lambda a: a.reshape(bd, t_new, a.shape[-1])
        hg = pad_rows(per_seq(o['hg']), HGRN_CHUNK)
        oa, st = _hgrn(hg, state_hgrn[l], HGRN_CHUNK, f"hgrn_sample_l{l}")
        oa = oa[:, :t_new].reshape(1, bd * t_new, A_W)
        iq_rows = o['iqb'].reshape(bd, t_new * IDX_HEADS, IDX_D)
        w_rows = jnp.broadcast_to(jnp.swapaxes(o['iwt'][0], 0, 1).reshape(bd, t_new * IDX_HEADS, 1),
                                  (bd, t_new * IDX_HEADS, LANES))
        keys, thr = _sample_index(l, idx_cache, page_table, iq_rows, w_rows, new_page(o['ikt']),
                                  math.gcd(SAMPLE_INDEX_GROUP, n_pages), min(TOPK_MAX, (past + t_new) // 4))
        lane = jnp.arange(B_W)
        r8 = jnp.arange(8)
        d_mask = (lane[None, :] // B_HD == r8[:, None])
        c_mask = (lane[None, :] // C_QK == r8[:, None])
        qd = jnp.where(d_mask[None, None], per_seq(o['bqb'])[:, :, None, :], 0).reshape(bd, 8 * t_new, B_W)
        qc = jnp.where(c_mask[None, None], per_seq(o['cqb'])[:, :, None, :], 0).reshape(bd, 8 * t_new, C_W)
        new_kv = [new_page(o[k]) for k in ('bkt', 'bvt32', 'ckt', 'cvt32')]
        ob, oc = _sample_attn(l, caches, page_table, keys, thr, qd, qc, new_kv, diff_lambda[l], group)
        ob = ob[:, :t_new].reshape(1, bd * t_new, B_W)
        oc = oc[:, :t_new].reshape(1, bd * t_new, C_W)
        xs = _outproj(l, xs, oa, ob, oc, o['gate'], w_out[l], hgrn_onorm_w[l], diff_subln_w[l], bd * t_new)
        outs['s_bk'].append(tok_major(o['bkt'], B_HEADS, B_HD).reshape(bd, t_new, B_HEADS, B_HD))
        outs['s_bv'].append(tok_major(o['bvt32'], B_HEADS, B_HD).reshape(bd, t_new, B_HEADS, B_HD))
        outs['s_ik'].append(tok_major(o['ikt'], IDX_D).reshape(bd, t_new, IDX_D))
        outs['s_ck'].append(tok_major(o['ckt'], C_HEADS, 2, C_QK).reshape(bd, t_new, C_HEADS, 2, C_QK))
        outs['s_cv'].append(tok_major(o['cvt32'], C_HEADS, C_DV).reshape(bd, t_new, C_HEADS, C_DV))
        outs['s_st'].append(st)

    stk = {k: jnp.stack(v) for k, v in outs.items() if v}

    def layers_tok_major(name, *feat):
        a = prompt_kv[name]
        return tok_major(a.reshape(depth * bn, a.shape[2], s), *feat).reshape((depth, bn, s) + feat)

    return (xp, xs.reshape(bd, t_new, d),
            layers_tok_major('bkt', B_HEADS, B_HD), layers_tok_major('bvt32', B_HEADS, B_HD),
            layers_tok_major('ikt', IDX_D), layers_tok_major('ckt', C_HEADS, 2, C_QK),
            layers_tok_major('cvt32', C_HEADS, C_DV), stk['p_st'],
            stk['s_bk'], stk['s_bv'], stk['s_ik'], stk['s_ck'], stk['s_cv'], stk['s_st'])
```

```python
import functools
import math

import numpy as np
import jax
import jax.numpy as jnp
from jax import lax
from jax.experimental import pallas as pl
from jax.experimental.pallas import tpu as pltpu

F32 = jnp.float32
BF16 = jnp.bfloat16
I32 = jnp.int32
I16 = jnp.int16

D_MODEL = 1024
A_W, A_HEADS, A_DK = 256, 4, 64
B_W, B_HEADS, B_HD = 384, 6, 64
IDX_HEADS, IDX_D = 8, 64
C_W, C_HEADS, C_DV, C_QK = 384, 4, 96, 48
TOPK_MAX = 256
PAGE_SIZE = 128
ROPE_THETA = 500000.0
EPS = 1e-6
NEG_BIG = -1e30
LOG2E = math.log2(math.e)
HGRN_CHUNK = 64
LANES = 128
INT_MIN = -(2 ** 31)
VMEM_LIMIT = 56 * 1024 * 1024
PROMPT_TILE = 256
SAMPLE_PAGE_GROUP = 16
SAMPLE_INDEX_GROUP = 32
HGRN_SEQS_PER_STEP = 2

_OFF_A = 0
_OFF_BQ, _OFF_BK, _OFF_BV, _OFF_BG = 1024, 1408, 1792, 2176
_OFF_IQ, _OFF_IK, _OFF_IW = 2560, 3072, 3200
_OFF_CQ, _OFF_CK, _OFF_CV, _OFF_CG = 3328, 3712, 4096, 4480
_N_PACK = 4864


def _dot(a, b):
    return jnp.dot(a, b, preferred_element_type=F32)


def _dot_nt(a, b):
    return lax.dot_general(a, b, (((1,), (1,)), ((), ())), preferred_element_type=F32)


def _pack_w_in(w):
    i_k = w[:, 3072:3136]
    i_w = w[:, 3136:3144]
    pad = jnp.zeros((w.shape[0], LANES - IDX_HEADS), w.dtype)
    packed = jnp.concatenate([w[:, :3072], i_k, i_k, i_w, pad, w[:, 3144:]], axis=1).astype(BF16)
    wv_t = jnp.concatenate([w[:, 1792:2176], w[:, 3912:4296]], axis=1).T.astype(BF16)
    return packed, wv_t


def _rope_tables(pos, d, n_rep):
    rot = d // 4
    half = rot // 2
    inv = jnp.power(ROPE_THETA, -2.0 * jnp.arange(half, dtype=F32) / rot)
    ang = pos.astype(F32)[:, None] * inv[None, :]
    cos, sin = jnp.cos(ang), jnp.sin(ang)
    t = pos.shape[0]
    one = jnp.ones((t, d - rot), F32)
    zero = jnp.zeros((t, d - half), F32)
    c = jnp.concatenate([cos, cos, one], axis=1)
    sa = jnp.concatenate([-sin, zero], axis=1)
    sb = jnp.concatenate([jnp.zeros((t, half), F32), sin, jnp.zeros((t, d - rot), F32)], axis=1)
    return jnp.stack([jnp.tile(c, (1, n_rep)), jnp.tile(sa, (1, n_rep)), jnp.tile(sb, (1, n_rep))])


def _block_diag_ones(width, group):
    g = np.arange(width) // group
    return jnp.asarray((g[:, None] == g[None, :]).astype(np.float32), dtype=BF16)


def _pad_perm():
    p = np.zeros((C_W, C_HEADS * LANES), np.float32)
    for j in range(C_W):
        p[j, (j // C_DV) * LANES + (j % C_DV)] = 1.0
    return jnp.asarray(p, dtype=BF16)


def _rope(x, tab_ref, width, half):
    c = tab_ref[0, :, :width]
    sa = tab_ref[1, :, :width]
    sb = tab_ref[2, :, :width]
    return x * c + pltpu.roll(x, width - half, 1) * sa + pltpu.roll(x, half, 1) * sb


def _head_rms(x, m_ref, group, w_row):
    ms = _dot((x * x).astype(BF16), m_ref[...]) * (1.0 / group)
    return x * lax.rsqrt(ms + EPS) * w_row


def _inproj_kernel(layer, n_carried, x_ref, nw_ref, w_ref, wvt_ref, lbl_ref, t64_ref, t48_ref, m64_ref, m48_ref,
                   ppad_ref, bqn_ref, bkn_ref, cqn_ref, ckn_ref, *rest):
    (hg_ref, gate_ref, bkt_ref, bvt32_ref, ikt_ref, ckt_ref, cvt32_ref,
     bqb_ref, bkb_ref, bvt_ref, iqb_ref, ikd_ref, iwt_ref, cqb_ref, cqp_ref, ckp_ref, cvt_ref) = rest[n_carried:]
    xf = x_ref[0]
    y = xf * lax.rsqrt(jnp.mean(xf * xf, axis=-1, keepdims=True) + EPS)
    h = (y * nw_ref[...]).astype(BF16)

    def seg(a, b):
        return _dot(h, w_ref[:, a:b])

    lg = lbl_ref[...]
    e = jnp.exp(lg - jnp.max(lg, axis=0, keepdims=True))
    soft = e / jnp.sum(e, axis=0, keepdims=True)
    lb = jnp.zeros((1, A_W), F32)
    for i in range(1, layer + 1):
        lb = lb + soft[i:i + 1, :]
    fz = seg(_OFF_A + 256, _OFF_A + 512)
    en = jnp.exp(-jnp.abs(fz))
    r = 1.0 / (1.0 + en)
    pos_side = fz >= 0
    sig_p = jnp.where(pos_side, r, en * r)
    sig_n = jnp.where(pos_side, en * r, r)
    hg_ref[0, :, 0:256] = seg(_OFF_A, _OFF_A + 256)
    hg_ref[0, :, 256:512] = (1.0 - lb) * sig_n
    hg_ref[0, :, 512:768] = jnp.log(lb + (1.0 - lb) * sig_p)
    hg_ref[0, :, 768:1024] = seg(_OFF_A + 512, _OFF_A + 768)

    def silu(g):
        return g / (1.0 + jnp.exp(-g))

    gate_ref[0, :, 0:256] = silu(seg(_OFF_A + 768, _OFF_A + 1024))
    gate_ref[0, :, 256:640] = silu(seg(_OFF_BG, _OFF_BG + B_W))
    gate_ref[0, :, 640:1024] = silu(seg(_OFF_CG, _OFF_CG + C_W))

    bq = _rope(_head_rms(seg(_OFF_BQ, _OFF_BQ + B_W), m64_ref, B_HD, bqn_ref[...]), t64_ref, B_W, B_HD // 8)
    bqb_ref[0] = (bq * (B_HD ** -0.5 * LOG2E)).astype(BF16)
    bk = _rope(_head_rms(seg(_OFF_BK, _OFF_BK + B_W), m64_ref, B_HD, bkn_ref[...]), t64_ref, B_W, B_HD // 8)
    bkt_ref[...] = bk.T
    bkb_ref[0] = bk.astype(BF16)
    bvt = _dot_nt(wvt_ref[0:B_W, :], h)
    bvt32_ref[...] = bvt
    bvt_ref[0, 0] = bvt.astype(BF16)

    iq = _rope(seg(_OFF_IQ, _OFF_IQ + IDX_HEADS * IDX_D), t64_ref, IDX_HEADS * IDX_D, IDX_D // 8)
    iqb_ref[0] = (iq * (IDX_D ** -0.5)).astype(BF16)
    ikd = _rope(seg(_OFF_IK, _OFF_IK + LANES), t64_ref, LANES, IDX_D // 8)
    ikt_ref[...] = ikd.T[0:IDX_D, :]
    ikd_ref[0] = ikd.astype(BF16)
    iw = seg(_OFF_IW, _OFF_IW + LANES) * (IDX_HEADS ** -0.5)
    iwt_ref[0] = iw.T[0:IDX_HEADS, :]

    cq = _rope(_head_rms(seg(_OFF_CQ, _OFF_CQ + C_W), m48_ref, C_QK, cqn_ref[...]), t48_ref, C_W, C_QK // 8)
    cqb = (cq * (C_QK ** -0.5 * LOG2E)).astype(BF16)
    cqb_ref[0] = cqb
    cqp_ref[0] = _dot(cqb, ppad_ref[...]).astype(BF16)
    ck = _rope(_head_rms(seg(_OFF_CK, _OFF_CK + C_W), m48_ref, C_QK, ckn_ref[...]), t48_ref, C_W, C_QK // 8)
    ckt_ref[...] = ck.T
    ckp_ref[0] = _dot(ck.astype(BF16), ppad_ref[...]).astype(BF16)
    cvt = _dot_nt(wvt_ref[B_W:B_W + C_W, :], h)
    cvt32_ref[...] = cvt
    cvt_ref[0, 0] = cvt.astype(BF16)


_STACKED_OUTPUTS = ('bkt', 'bvt32', 'ikt', 'ckt', 'cvt32')


def _inproj(layer, x, pos, norm_w, w_packs, lb_logits, bqn, bkn, cqn, ckn, tm, depth=None, stacked=None):
    w_pack, wv_t = w_packs
    bn, t, d = x.shape
    nt = t // tm
    t64 = _rope_tables(pos, IDX_D, IDX_HEADS)
    t48 = _rope_tables(pos, C_QK, 2 * C_HEADS)
    m64 = _block_diag_ones(B_W, B_HD)
    m48 = _block_diag_ones(C_W, C_QK)
    ppad = _pad_perm()

    def tok(width, dtype):
        return jax.ShapeDtypeStruct((bn, t, width), dtype), pl.BlockSpec((1, tm, width), lambda i, b: (b, i, 0))

    def trf(rows):
        if depth is None:
            return (jax.ShapeDtypeStruct((bn, rows, t), F32), pl.BlockSpec((None, rows, tm), lambda i, b: (b, 0, i)))
        return (jax.ShapeDtypeStruct((depth, bn, rows, t), F32),
                pl.BlockSpec((None, None, rows, tm), lambda i, b: (layer, b, 0, i)))

    def tr(rows, dtype):
        return (jax.ShapeDtypeStruct((bn, nt, rows, tm), dtype),
                pl.BlockSpec((1, 1, rows, tm), lambda i, b: (b, i, 0, 0)))

    outs = dict(
        hg=tok(1024, F32), gate=tok(1024, F32), bkt=trf(B_W), bvt32=trf(B_W), ikt=trf(IDX_D), ckt=trf(C_W), cvt32=trf(C_W),
        bqb=tok(B_W, BF16), bkb=tok(B_W, BF16), bvt=tr(B_W, BF16), iqb=tok(IDX_HEADS * IDX_D, BF16),
        ikd=tok(LANES, BF16),
        iwt=(jax.ShapeDtypeStruct((bn, IDX_HEADS, t), F32), pl.BlockSpec((1, IDX_HEADS, tm), lambda i, b: (b, 0, i))),
        cqb=tok(C_W, BF16), cqp=tok(C_HEADS * LANES, BF16), ckp=tok(C_HEADS * LANES, BF16), cvt=tr(C_W, BF16),
    )
    names = list(outs)

    def const(shape):
        nd = len(shape)
        return pl.BlockSpec(shape, lambda i, b: (0,) * nd)

    in_specs = [
        pl.BlockSpec((1, tm, d), lambda i, b: (b, i, 0)),
        const((1, d)),
        const((d, _N_PACK)),
        const((B_W + C_W, d)),
        const(lb_logits.shape),
        pl.BlockSpec((3, tm, IDX_HEADS * IDX_D), lambda i, b: (0, i, 0)),
        pl.BlockSpec((3, tm, C_W), lambda i, b: (0, i, 0)),
        const((B_W, B_W)), const((C_W, C_W)), const((C_W, C_HEADS * LANES)),
        const((1, B_W)), const((1, B_W)), const((1, C_W)), const((1, C_W)),
    ]
    carried = [] if stacked is None else [stacked[n] for n in _STACKED_OUTPUTS]
    aliases = {len(in_specs) + j: names.index(n) for j, n in enumerate(_STACKED_OUTPUTS)} if carried else {}
    res = pl.pallas_call(
        functools.partial(_inproj_kernel, layer, len(carried)),
        grid=(nt, bn),
        in_specs=in_specs + [pl.BlockSpec(memory_space=pl.ANY)] * len(carried),
        out_specs=[outs[n][1] for n in names],
        out_shape=[outs[n][0] for n in names],
        input_output_aliases=aliases,
        compiler_params=pltpu.CompilerParams(dimension_semantics=("arbitrary", "arbitrary"),
                                             vmem_limit_bytes=VMEM_LIMIT),
        name=f"inproj_l{layer}_t{t}",
    )(x, norm_w.reshape(1, d), w_pack, wv_t, lb_logits, t64, t48, m64, m48, ppad,
      jnp.tile(bqn, B_HEADS).reshape(1, B_W), jnp.tile(bkn, B_HEADS).reshape(1, B_W),
      jnp.tile(cqn, 2 * C_HEADS).reshape(1, C_W), jnp.tile(ckn, 2 * C_HEADS).reshape(1, C_W), *carried)
    return dict(zip(names, res))


_HGRN_LEVELS = (1, 2, 4, 8, 16, 32)


def _hgrn_tables():
    c = HGRN_CHUNK
    t = np.arange(c)[:, None]
    u = np.arange(c)[None, :]
    mats = [(u <= t), (u > t)]
    masks = [(t == u)]
    for h in _HGRN_LEVELS:
        r = (t // (2 * h)) * (2 * h) + h - 1
        right = (t % (2 * h)) >= h
        mats.append(np.where(right, (u > r) & (u <= t), (u > t) & (u <= r)))
        masks.append((t // (2 * h) == u // (2 * h)) & right & ((u % (2 * h)) < h))
    tall = np.tile(np.concatenate(mats, axis=0).astype(np.float32), (1, 3))
    mk = np.stack([np.tile(m, (1, 2)) for m in masks]).astype(np.float32)
    return jnp.asarray(tall, dtype=BF16), jnp.asarray(mk)


def _pair_expand(a):
    lane = lax.broadcasted_iota(I32, a.shape, 1)
    zero = jnp.zeros_like(a)
    return jnp.concatenate([jnp.where(lane < A_DK, a, zero), jnp.where(lane >= A_DK, a, zero)], axis=0)


def _hgrn_kernel(nchunk, nseq, hg_ref, tall_ref, mk_ref, s0_ref, o_ref, sf_ref, st_ref):
    c = HGRN_CHUNK

    @pl.when(pl.program_id(1) == 0)
    def _():
        st_ref[...] = s0_ref[...]

    row = lax.broadcasted_iota(I32, (2 * A_DK, 2 * A_DK), 0)
    col = lax.broadcasted_iota(I32, (2 * A_DK, 2 * A_DK), 1)
    same_head = (row // A_DK) == (col // A_DK)
    zpad = jnp.zeros((c, 2 * A_DK), BF16)

    units = [(sq, p) for sq in range(nseq) for p in range(2)]
    n_lev = len(_HGRN_LEVELS)
    for ci in range(nchunk):
        rows = slice(ci * c, (ci + 1) * c)
        exs = []
        for sq in range(nseq):
            lf = hg_ref[sq, rows, 512:768]
            l1 = lf.astype(BF16)
            r1 = lf - l1.astype(F32)
            l2 = r1.astype(BF16)
            l3 = (r1 - l2.astype(F32)).astype(BF16)
            exs.append(_dot(tall_ref[...], jnp.concatenate([l1, l2, l3], axis=0)))
        exs = [jnp.exp(e) for e in exs]
        inter, level, upd, vexp, states = [], [], [], [], []
        for sq, p in units:
            ex = exs[sq]
            lanes = slice(p * 128, (p + 1) * 128)
            qp = hg_ref[sq, rows, lanes]
            kp = hg_ref[sq, rows, 256 + p * 128:256 + (p + 1) * 128]
            vp = hg_ref[sq, rows, 768 + p * 128:768 + (p + 1) * 128].astype(BF16)
            st = st_ref[sq, p]
            states.append(st)
            vexp.append(_pair_expand(vp))
            inter.append(_dot_nt((qp * ex[0:c, lanes]).astype(BF16), st.astype(BF16)))
            lv = [_dot_nt(qp.astype(BF16), _pair_expand(kp.astype(BF16)))]
            for li in range(n_lev):
                xl = ex[(2 + li) * c:(3 + li) * c, lanes]
                lv.append(_dot_nt((qp * xl).astype(BF16), _pair_expand((kp * xl).astype(BF16))))
            level.append(lv)
            kh = (kp * ex[c:2 * c, lanes]).astype(BF16)
            vt = jnp.concatenate([vp, zpad], axis=0).astype(F32).T.astype(BF16)
            upd.append(_dot(vt, jnp.concatenate([kh, zpad], axis=0)))
        scores = []
        for lv in level:
            sc = mk_ref[0] * lv[0]
            for li in range(n_lev):
                sc = sc + mk_ref[1 + li] * lv[1 + li]
            scores.append(sc.astype(BF16))
        intra = [_dot(sc, ve) for sc, ve in zip(scores, vexp)]
        for u, (sq, p) in enumerate(units):
            lanes = slice(p * 128, (p + 1) * 128)
            o_ref[sq, rows, lanes] = inter[u] + intra[u]
            decay = exs[sq][c - 1:c, lanes]
            st_ref[sq, p] = jnp.where(same_head, states[u] * decay + upd[u], 0.0)
    sf_ref[...] = st_ref[...]


def _state_to_pairs(s):
    bn = s.shape[0]
    st = jnp.swapaxes(s, -1, -2).reshape(bn, 2, 2, A_DK, A_DK)
    z = jnp.zeros((bn, 2, A_DK, A_DK), s.dtype)
    top = jnp.concatenate([st[:, :, 0], z], axis=-1)
    bot = jnp.concatenate([z, st[:, :, 1]], axis=-1)
    return jnp.concatenate([top, bot], axis=-2)


def _pairs_to_state(sp):
    bn = sp.shape[0]
    h0 = sp[:, :, :A_DK, :A_DK]
    h1 = sp[:, :, A_DK:, A_DK:]
    return jnp.swapaxes(jnp.stack([h0, h1], axis=2).reshape(bn, A_HEADS, A_DK, A_DK), -1, -2)


def _hgrn(hg, s0, tb, name):
    bn, t, _ = hg.shape
    nseq = HGRN_SEQS_PER_STEP
    assert bn % nseq == 0
    tall, mk = _hgrn_tables()
    o, sf = pl.pallas_call(
        functools.partial(_hgrn_kernel, tb // HGRN_CHUNK, nseq),
        grid=(bn // nseq, t // tb),
        in_specs=[pl.BlockSpec((nseq, tb, 1024), lambda b, j: (b, j, 0)),
                  pl.BlockSpec(tall.shape, lambda b, j: (0, 0)),
                  pl.BlockSpec(mk.shape, lambda b, j: (0, 0, 0)),
                  pl.BlockSpec((nseq, 2, 128, 128), lambda b, j: (b, 0, 0, 0))],
        out_specs=[pl.BlockSpec((nseq, tb, A_W), lambda b, j: (b, j, 0)),
                   pl.BlockSpec((nseq, 2, 128, 128), lambda b, j: (b, 0, 0, 0))],
        out_shape=[jax.ShapeDtypeStruct((bn, t, A_W), F32), jax.ShapeDtypeStruct((bn, 2, 128, 128), F32)],
        scratch_shapes=[pltpu.VMEM((nseq, 2, 128, 128), F32)],
        compiler_params=pltpu.CompilerParams(dimension_semantics=("arbitrary", "arbitrary"),
                                             vmem_limit_bytes=VMEM_LIMIT),
        name=name,
    )(hg, tall, mk, _state_to_pairs(s0))
    return o, _pairs_to_state(sf)


DSA_TQ = 256


def _key_of(score):
    bits = lax.bitcast_convert_type(score, I32)
    key = bits ^ ((bits >> 31) & 0x7FFFFFFF)
    return jnp.where(score == 0.0, 0, key)


def _count_rows(pred_fn, nk, ch, width):
    def body(kc, acc):
        hit = jnp.where(pred_fn(kc), 1, 0)
        return acc + jnp.sum(hit.reshape(ch // 8, 8, width), axis=0)

    acc = lax.fori_loop(0, nk, body, jnp.zeros((8, width), I32))
    return jnp.sum(acc, axis=0, keepdims=True)


def _select_topk(keys_ref, top_ref, nk, ch, width, n_keep, pos_bits):
    def chunk(kc):
        return keys_ref[pl.ds(kc * ch, ch), :]

    def kpos(kc):
        return kc * ch + lax.broadcasted_iota(I32, (ch, width), 0)

    def count16(c16):
        def body(kc, acc):
            x = top_ref[pl.ds(kc * ch, ch), :]
            hit = jnp.where(x >= c16, jnp.ones_like(x), jnp.zeros_like(x)).reshape(ch // 16, 16, width)
            for j in range(ch // 16):
                acc = acc + hit[j]
            return acc

        acc = lax.fori_loop(0, nk, body, jnp.zeros((16, width), I16))
        return jnp.sum(acc.astype(I32), axis=0, keepdims=True)

    def top_step(i, tu):
        cand = tu | jnp.left_shift(jnp.int32(1), 31 - i)
        cnt = count16(((cand ^ INT_MIN) >> 16).astype(I16))
        return jnp.where(cnt >= n_keep, cand, tu)

    tu = lax.fori_loop(0, 16, top_step, jnp.zeros((1, width), I32))

    h16 = (tu ^ INT_MIN) >> 16

    def bake(kc, carry):
        k = chunk(kc)
        t = k >> 16
        low = (k & 0xFFFF) - 32768
        top_ref[pl.ds(kc * ch, ch), :] = jnp.where(t > h16, 32767, jnp.where(t == h16, low, -32768)).astype(I16)
        return carry

    lax.fori_loop(0, nk, bake, 0)

    def low_step(i, tu):
        cand = tu | jnp.left_shift(jnp.int32(1), 31 - i)
        cnt = count16(((cand & 0xFFFF) - 32768).astype(I16))
        return jnp.where(cnt >= n_keep, cand, tu)

    thr = lax.fori_loop(16, 32, low_step, tu) ^ INT_MIN
    cnt_ge = _count_rows(lambda kc: chunk(kc) >= thr, nk, ch, width)
    excess = jnp.where(thr > INT_MIN, cnt_ge - n_keep, 0)

    @pl.when(jnp.max(excess) > 0)
    def _():
        need = n_keep - _count_rows(lambda kc: chunk(kc) > thr, nk, ch, width)

        def pos_step(i, c):
            cand = c | jnp.left_shift(jnp.int32(1), pos_bits - 1 - i)
            before = _count_rows(lambda kc: (chunk(kc) == thr) & (kpos(kc) < cand), nk, ch, width)
            return jnp.where(before <= need - 1, cand, c)

        cut = lax.fori_loop(0, pos_bits, pos_step, jnp.zeros((1, width), I32))

        def drop(kc, carry):
            k = chunk(kc)
            late_tie = (k == thr) & (kpos(kc) > cut) & (excess > 0)
            keys_ref[pl.ds(kc * ch, ch), :] = jnp.where(late_tie, INT_MIN, k)
            return carry

        lax.fori_loop(0, nk, drop, 0)

    return jnp.maximum(thr, INT_MIN + 1)


def _lane_halves(x):
    lane = lax.broadcasted_iota(I32, x.shape, 1)
    zero = jnp.zeros_like(x)
    return jnp.concatenate([jnp.where(lane < B_HD, x, zero), jnp.where(lane >= B_HD, x, zero)], axis=0)


def _softmax_updates_t(scores, values_t, m_ref, l_ref, acc_ref):
    alphas, probs = [], []
    for j, s in enumerate(scores):
        m_old = m_ref[j, 0:1, :]
        m_new = jnp.maximum(m_old, jnp.max(s, axis=0, keepdims=True))
        alpha = jnp.exp2(m_old - m_new)
        e = jnp.exp2(s - m_new)
        l_ref[j, 0:1, :] = alpha * l_ref[j, 0:1, :] + jnp.sum(e, axis=0, keepdims=True)
        m_ref[j, 0:1, :] = m_new
        alphas.append(alpha)
        probs.append(e.astype(BF16))
    updates = [_dot(v_t, p) for v_t, p in zip(values_t, probs)]
    for j, (alpha, upd) in enumerate(zip(alphas, updates)):
        acc_ref[j] = alpha * acc_ref[j] + upd


def _dsa_prompt_kernel(n_keep, ch, pos_bits, iqb_ref, iwt_ref, ikd_ref, bqb_ref, bkb_ref, bvt_ref, o_ref,
                       keys_ref, top_ref, m_ref, l_ref, acc_ref):
    tq = DSA_TQ
    qb = pl.program_id(1)
    nk = (qb * tq + tq + ch - 1) // ch
    qpos = qb * tq + lax.broadcasted_iota(I32, (1, tq), 1)

    iq = iqb_ref[0]
    iq_rows = jnp.concatenate([_lane_halves(iq[:, j * LANES:(j + 1) * LANES]) for j in range(IDX_HEADS // 2)], axis=0)
    wt = iwt_ref[0]

    def index_chunk(kc, carry):
        hc = ch // 2
        raw = [_dot_nt(ikd_ref[0, pl.ds(kc * ch + i * hc, hc), :], iq_rows) for i in range(2)]
        for i, s in enumerate(raw):
            acc = jnp.zeros((hc, tq), F32)
            for h in range(IDX_HEADS):
                acc = acc + wt[h:h + 1, :] * jnp.maximum(s[:, h * tq:(h + 1) * tq], 0.0)
            kp = kc * ch + i * hc + lax.broadcasted_iota(I32, (hc, tq), 0)
            key = jnp.where(kp <= qpos, _key_of(acc), INT_MIN)
            keys_ref[pl.ds(kc * ch + i * hc, hc), :] = key
            top_ref[pl.ds(kc * ch + i * hc, hc), :] = (key >> 16).astype(I16)
        return carry

    lax.fori_loop(0, nk, index_chunk, 0)

    @pl.when(nk % 2 == 1)
    def _():
        keys_ref[pl.ds(nk * ch, ch), :] = jnp.full((ch, tq), INT_MIN, I32)
        top_ref[pl.ds(nk * ch, ch), :] = jnp.full((ch, tq), INT_MIN >> 16, I16)

    thr = _select_topk(keys_ref, top_ref, (nk + 1) // 2, 2 * ch, tq, n_keep, pos_bits)

    n_pairs = B_HEADS // 2
    bq = bqb_ref[0]
    q_rows = [_lane_halves(bq[:, p * LANES:(p + 1) * LANES]) for p in range(n_pairs)]
    m_ref[...] = jnp.full(m_ref.shape, NEG_BIG, F32)
    l_ref[...] = jnp.zeros(l_ref.shape, F32)
    acc_ref[...] = jnp.zeros(acc_ref.shape, F32)

    def attend(kc, carry):
        sel = keys_ref[pl.ds(kc * ch, ch), :] >= thr
        raw = [_dot_nt(bkb_ref[0, pl.ds(kc * ch, ch), p * LANES:(p + 1) * LANES], q_rows[p])
               for p in range(n_pairs)]
        scores = [jnp.concatenate([jnp.where(sel, s[:, :tq], NEG_BIG), jnp.where(sel, s[:, tq:], NEG_BIG)], axis=1)
                  for s in raw]
        values_t = [bvt_ref[0, kc, p * LANES:(p + 1) * LANES, :] for p in range(n_pairs)]
        _softmax_updates_t(scores, values_t, m_ref, l_ref, acc_ref)
        return carry

    lax.fori_loop(0, nk, attend, 0)
    outs = []
    for p in range(n_pairs):
        inv = 1.0 / l_ref[p, 0:1, :]
        outs.append(acc_ref[p, 0:B_HD, 0:tq] * inv[:, 0:tq])
        outs.append(acc_ref[p, B_HD:, tq:] * inv[:, tq:])
    o_ref[0] = jnp.concatenate(outs, axis=0).T


def _dsa_prompt(o, ch):
    bn, s, _ = o['bkb'].shape
    n_keep = min(TOPK_MAX, s // 4)
    nb = s // DSA_TQ
    assert (s // ch) % 2 == 0, "the threshold search walks key chunks in pairs"
    n_pairs = B_HEADS // 2
    return pl.pallas_call(
        functools.partial(_dsa_prompt_kernel, n_keep, ch, s.bit_length()),
        grid=(bn, nb),
        in_specs=[pl.BlockSpec((1, DSA_TQ, IDX_HEADS * IDX_D), lambda b, q: (b, q, 0)),
                  pl.BlockSpec((1, IDX_HEADS, DSA_TQ), lambda b, q: (b, 0, q)),
                  pl.BlockSpec((1, s, LANES), lambda b, q: (b, 0, 0)),
                  pl.BlockSpec((1, DSA_TQ, B_W), lambda b, q: (b, q, 0)),
                  pl.BlockSpec((1, s, B_W), lambda b, q: (b, 0, 0)),
                  pl.BlockSpec((1, s // ch, B_W, ch), lambda b, q: (b, 0, 0, 0))],
        out_specs=pl.BlockSpec((1, DSA_TQ, B_W), lambda b, q: (b, q, 0)),
        out_shape=jax.ShapeDtypeStruct((bn, s, B_W), F32),
        scratch_shapes=[pltpu.VMEM((s, DSA_TQ), I32), pltpu.VMEM((s, DSA_TQ), I16),
                        pltpu.VMEM((n_pairs, 8, 2 * DSA_TQ), F32),
                        pltpu.VMEM((n_pairs, 8, 2 * DSA_TQ), F32), pltpu.VMEM((n_pairs, LANES, 2 * DSA_TQ), F32)],
        compiler_params=pltpu.CompilerParams(dimension_semantics=("arbitrary", "arbitrary"),
                                             vmem_limit_bytes=VMEM_LIMIT),
        name="dsa_prompt",
    )(o['iqb'], o['iwt'], o['ikd'], o['bqb'], o['bkb'], o['bvt'])


def _diff_lambda(dl, lam_init):
    a = jnp.sum(dl[0:1, :] * dl[1:2, :], axis=1, keepdims=True)
    b = jnp.sum(dl[2:3, :] * dl[3:4, :], axis=1, keepdims=True)
    return jnp.exp(a) - jnp.exp(b) + lam_init


def _diff_prompt_kernel(lam_init, ch, dl_ref, cqp_ref, ckp_ref, cvt_ref, o_ref, m_ref, l_ref, acc_ref):
    tq = DSA_TQ
    qb = pl.program_id(1)
    nk = (qb * tq + tq + ch - 1) // ch
    n_full = (qb * tq) // ch
    qpos = qb * tq + lax.broadcasted_iota(I32, (1, tq), 1)
    lam = _diff_lambda(dl_ref[...], lam_init)
    cq = cqp_ref[0]
    lane = lax.broadcasted_iota(I32, (tq, LANES), 1)
    q_rows = []
    for h in range(C_HEADS):
        qh = cq[:, h * LANES:(h + 1) * LANES]
        zero = jnp.zeros_like(qh)
        q_rows.append(jnp.concatenate([jnp.where(lane < C_QK, qh, zero),
                                       jnp.where((lane >= C_QK) & (lane < 2 * C_QK), qh, zero)], axis=0))
    m_ref[...] = jnp.full(m_ref.shape, NEG_BIG, F32)
    l_ref[...] = jnp.zeros(l_ref.shape, F32)
    acc_ref[...] = jnp.zeros(acc_ref.shape, F32)

    def attend(kc, masked):
        scores = [_dot_nt(ckp_ref[0, pl.ds(kc * ch, ch), h * LANES:(h + 1) * LANES], q_rows[h])
                  for h in range(C_HEADS)]
        if masked:
            vis = (kc * ch + lax.broadcasted_iota(I32, (ch, tq), 0)) <= qpos
            scores = [jnp.concatenate([jnp.where(vis, s[:, :tq], NEG_BIG), jnp.where(vis, s[:, tq:], NEG_BIG)],
                                      axis=1) for s in scores]
        values_t = [cvt_ref[0, kc, h * C_DV:(h + 1) * C_DV, :] for h in range(C_HEADS)]
        _softmax_updates_t(scores, values_t, m_ref, l_ref, acc_ref)

    def full_chunk(kc, carry):
        attend(kc, False)
        return carry

    def diag_chunk(kc, carry):
        attend(kc, True)
        return carry

    lax.fori_loop(0, n_full, full_chunk, 0)
    lax.fori_loop(n_full, nk, diag_chunk, 0)
    outs = []
    for h in range(C_HEADS):
        inv = 1.0 / l_ref[h, 0:1, :]
        acc = acc_ref[h]
        outs.append(acc[:, :tq] * inv[:, :tq] - lam * (acc[:, tq:] * inv[:, tq:]))
    o_ref[0] = jnp.concatenate(outs, axis=0).T


def _diff_prompt(layer, o, dl, ch):
    bn, s, _ = o['ckp'].shape
    lam_init = 0.8 - 0.6 * math.exp(-0.3 * layer)
    return pl.pallas_call(
        functools.partial(_diff_prompt_kernel, lam_init, ch),
        grid=(bn, s // DSA_TQ),
        in_specs=[pl.BlockSpec((4, C_QK), lambda b, q: (0, 0)),
                  pl.BlockSpec((1, DSA_TQ, C_HEADS * LANES), lambda b, q: (b, q, 0)),
                  pl.BlockSpec((1, s, C_HEADS * LANES), lambda b, q: (b, 0, 0)),
                  pl.BlockSpec((1, s // ch, C_W, ch), lambda b, q: (b, 0, 0, 0))],
        out_specs=pl.BlockSpec((1, DSA_TQ, C_W), lambda b, q: (b, q, 0)),
        out_shape=jax.ShapeDtypeStruct((bn, s, C_W), F32),
        scratch_shapes=[pltpu.VMEM((C_HEADS, 8, 2 * DSA_TQ), F32), pltpu.VMEM((C_HEADS, 8, 2 * DSA_TQ), F32),
                        pltpu.VMEM((C_HEADS, C_DV, 2 * DSA_TQ), F32)],
        compiler_params=pltpu.CompilerParams(dimension_semantics=("arbitrary", "arbitrary"),
                                             vmem_limit_bytes=VMEM_LIMIT),
        name="diff_prompt",
    )(dl, o['cqp'], o['ckp'], o['cvt'])


def _token_minor(cache):
    nd = cache.ndim
    c = jnp.transpose(cache, (0, 1) + tuple(range(3, nd)) + (2,))
    return c.reshape(c.shape[0], c.shape[1], -1, c.shape[-1])


def _page_specs(layer, n, group, width):
    def spec(j):
        return pl.BlockSpec((None, None, width, PAGE_SIZE), lambda b, g, pt: (layer, pt[b, g * group + j], 0, 0))

    return [spec(j) for _ in range(n) for j in range(group)]


def _rows_of_queries(k4, t_new):
    return jnp.concatenate([jnp.broadcast_to(k4[t:t + 1, :], (8, k4.shape[1])) for t in range(t_new)], axis=0)


def _select_topk_rows(keys_ref, lp, n_keep, pos_bits):
    cw = 16 * LANES

    def count(pred):
        acc = jnp.zeros((8, LANES), I32)
        for c0 in range(0, lp, cw):
            w = min(cw, lp - c0)
            hit = jnp.where(pred(keys_ref[0, :, c0:c0 + w], c0, w), 1, 0)
            for j in range(w // LANES):
                acc = acc + hit[:, j * LANES:(j + 1) * LANES]
        return jnp.sum(acc.astype(F32), axis=1, keepdims=True).astype(I32)

    def kpos(c0, w):
        return c0 + lax.broadcasted_iota(I32, (8, w), 1)

    def bit_step(i, tu):
        cand = tu | jnp.left_shift(jnp.int32(1), 31 - i)
        cnt = count(lambda k, c0, w: k >= (cand ^ INT_MIN))
        return jnp.where(cnt >= n_keep, cand, tu)

    thr = lax.fori_loop(0, 32, bit_step, jnp.zeros((8, 1), I32)) ^ INT_MIN
    excess = jnp.where(thr > INT_MIN, count(lambda k, c0, w: k >= thr) - n_keep, 0)

    @pl.when(jnp.max(excess) > 0)
    def _():
        need = n_keep - count(lambda k, c0, w: k > thr)

        def pos_step(i, c):
            cand = c | jnp.left_shift(jnp.int32(1), pos_bits - 1 - i)
            before = count(lambda k, c0, w: (k == thr) & (kpos(c0, w) < cand))
            return jnp.where(before <= need - 1, cand, c)

        cut = lax.fori_loop(0, pos_bits, pos_step, jnp.zeros((8, 1), I32))
        for c0 in range(0, lp, cw):
            w = min(cw, lp - c0)
            k = keys_ref[0, :, c0:c0 + w]
            late_tie = (k == thr) & (kpos(c0, w) > cut) & (excess > 0)
            keys_ref[0, :, c0:c0 + w] = jnp.where(late_tie, INT_MIN, k)

    return jnp.maximum(thr, INT_MIN + 1)


def _sample_index_kernel(group, n_groups, t_new, n_keep, pos_bits, pt_ref, *refs):
    pages = refs[:group]
    iq_ref, w_ref, iknew_ref, keys_ref, thr_ref = refs[group:]
    g = pl.program_id(1)
    gk = group * PAGE_SIZE
    past = n_groups * gk
    iq = iq_ref[0]
    wcol = w_ref[0][:, 0:1]

    def scores(kmat_t):
        s = jnp.maximum(_dot(iq, kmat_t), 0.0) * wcol
        return jnp.concatenate([jnp.sum(s[8 * t:8 * t + 8, :], axis=0, keepdims=True) for t in range(t_new)], axis=0)

    kcat = jnp.concatenate([p[...].astype(BF16) for p in pages], axis=1)
    key = _key_of(scores(kcat))
    key8 = jnp.concatenate([key] * (8 // t_new), axis=0)
    for gg in range(n_groups):
        @pl.when(g == gg)
        def _(gg=gg):
            keys_ref[0, :, gg * gk:(gg + 1) * gk] = key8

    @pl.when(g == n_groups - 1)
    def _():
        kn = _key_of(scores(iknew_ref[0]))
        row = lax.broadcasted_iota(I32, kn.shape, 0)
        lane = lax.broadcasted_iota(I32, kn.shape, 1)
        kn = jnp.where((lane <= row) & (lane < t_new), kn, INT_MIN)
        keys_ref[0, :, past:past + LANES] = jnp.concatenate([kn] * (8 // t_new), axis=0)
        thr = _select_topk_rows(keys_ref, past + LANES, n_keep, pos_bits)
        thr_ref[0] = jnp.broadcast_to(thr, (8, LANES))


def _sample_index(layer, cache_idx_k, page_table, iq_rows, w_rows, iknew, group, n_keep):
    bd, n_pages = page_table.shape
    n_groups = n_pages // group
    t_new = iq_rows.shape[1] // 8
    past = n_pages * PAGE_SIZE
    lp = past + LANES
    grid_spec = pltpu.PrefetchScalarGridSpec(
        num_scalar_prefetch=1,
        grid=(bd, n_groups),
        in_specs=_page_specs(layer, 1, group, IDX_D) + [
            pl.BlockSpec((1, 8 * t_new, IDX_D), lambda b, g, pt: (b, 0, 0)),
            pl.BlockSpec((1, 8 * t_new, LANES), lambda b, g, pt: (b, 0, 0)),
            pl.BlockSpec((1, IDX_D, PAGE_SIZE), lambda b, g, pt: (b, 0, 0))],
        out_specs=[pl.BlockSpec((1, 8, lp), lambda b, g, pt: (b, 0, 0)),
                   pl.BlockSpec((1, 8, LANES), lambda b, g, pt: (b, 0, 0))],
    )
    return pl.pallas_call(
        functools.partial(_sample_index_kernel, group, n_groups, t_new, n_keep, lp.bit_length()),
        grid_spec=grid_spec,
        out_shape=[jax.ShapeDtypeStruct((bd, 8, lp), I32), jax.ShapeDtypeStruct((bd, 8, LANES), I32)],
        compiler_params=pltpu.CompilerParams(dimension_semantics=("arbitrary", "arbitrary"),
                                             vmem_limit_bytes=VMEM_LIMIT),
        name=f"sample_index_l{layer}",
    )(page_table, *([cache_idx_k] * group), iq_rows, w_rows, iknew)


def _softmax_step(s, valid, v_t, m_ref, l_ref, acc_ref):
    if valid is not None:
        s = jnp.where(valid, s, NEG_BIG)
    m_old = m_ref[:, 0:1]
    m_new = jnp.maximum(m_old, jnp.max(s, axis=1, keepdims=True))
    alpha = jnp.exp2(m_old - m_new)
    e = jnp.exp2(s - m_new)
    if valid is not None:
        e = jnp.where(valid, e, 0.0)
    l_new = alpha * l_ref[:, 0:1] + jnp.sum(e, axis=1, keepdims=True)
    acc_ref[...] = alpha * acc_ref[...] + _dot_nt(e.astype(BF16), v_t)
    m_ref[...] = jnp.broadcast_to(m_new, m_ref.shape)
    l_ref[...] = jnp.broadcast_to(l_new, l_ref.shape)


def _sample_attn_kernel(group, n_groups, t_new, lam_init, pt_ref, *refs):
    n = group
    dk_pages, dv_pages, ck_pages, cv_pages = refs[0:n], refs[n:2 * n], refs[2 * n:3 * n], refs[3 * n:4 * n]
    (keys_ref, keysn_ref, thr_ref, qd_ref, qc_ref, dkn_ref, dvn_ref, ckn_ref, cvn_ref, dl_ref,
     ob_ref, oc_ref, md_ref, ld_ref, ad_ref, mc_ref, lc_ref, ac_ref) = refs[4 * n:]
    g = pl.program_id(1)
    rows = 8 * t_new

    @pl.when(g == 0)
    def _():
        for m_ref, l_ref, a_ref in ((md_ref, ld_ref, ad_ref), (mc_ref, lc_ref, ac_ref)):
            m_ref[...] = jnp.full(m_ref.shape, NEG_BIG, F32)
            l_ref[...] = jnp.zeros(l_ref.shape, F32)
            a_ref[...] = jnp.zeros(a_ref.shape, F32)

    def cat(pages):
        return jnp.concatenate([p[...].astype(BF16) for p in pages], axis=1)

    thr = _rows_of_queries(thr_ref[0], t_new)[:, 0:1]
    qd = qd_ref[0]
    qc = qc_ref[0]
    sel = _rows_of_queries(keys_ref[0], t_new) >= thr
    _softmax_step(_dot(qd, cat(dk_pages)), sel, cat(dv_pages), md_ref, ld_ref, ad_ref)
    _softmax_step(_dot(qc, cat(ck_pages)), None, cat(cv_pages), mc_ref, lc_ref, ac_ref)

    @pl.when(g == n_groups - 1)
    def _():
        row = lax.broadcasted_iota(I32, (rows, LANES), 0)
        lane = lax.broadcasted_iota(I32, (rows, LANES), 1)
        causal = (lane <= row // 8) & (lane < t_new)
        seln = causal & (_rows_of_queries(keysn_ref[0], t_new) >= thr)
        _softmax_step(_dot(qd, dkn_ref[0]), seln, dvn_ref[0], md_ref, ld_ref, ad_ref)
        _softmax_step(_dot(qc, ckn_ref[0]), causal, cvn_ref[0], mc_ref, lc_ref, ac_ref)

        r = lax.broadcasted_iota(I32, (rows, B_W), 0) % 8
        ln = lax.broadcasted_iota(I32, (rows, B_W), 1)
        od = jnp.where(ln // B_HD == r, ad_ref[...] / ld_ref[:, 0:1], 0.0)
        lam = _diff_lambda(dl_ref[...], lam_init)
        coef = jnp.where(r % 2 == 0, 1.0, -lam)
        oc = jnp.where(ln // C_DV == r // 2, coef * (ac_ref[...] / lc_ref[:, 0:1]), 0.0)
        pad = jnp.zeros((8 - t_new, B_W), F32)
        ob_ref[0] = jnp.concatenate([jnp.sum(od[8 * t:8 * t + 8], axis=0, keepdims=True) for t in range(t_new)] + [pad],
                                    axis=0)
        oc_ref[0] = jnp.concatenate([jnp.sum(oc[8 * t:8 * t + 8], axis=0, keepdims=True) for t in range(t_new)] + [pad],
                                    axis=0)


def _sample_attn(layer, caches, page_table, keys, thr, qd, qc, new_kv, dl, group):
    bd, n_pages = page_table.shape
    n_groups = n_pages // group
    rows = qd.shape[1]
    t_new = rows // 8
    gk = group * PAGE_SIZE
    past = n_pages * PAGE_SIZE
    lam_init = 0.8 - 0.6 * math.exp(-0.3 * layer)

    def per_seq(shape):
        nd = len(shape)
        return pl.BlockSpec((1,) + shape, lambda b, g, pt: (b,) + (0,) * nd)

    grid_spec = pltpu.PrefetchScalarGridSpec(
        num_scalar_prefetch=1,
        grid=(bd, n_groups),
        in_specs=_page_specs(layer, 4, group, B_W) + [
            pl.BlockSpec((1, 8, gk), lambda b, g, pt: (b, 0, g)),
            pl.BlockSpec((1, 8, LANES), lambda b, g, pt: (b, 0, past // LANES)),
            per_seq((8, LANES)), per_seq((rows, B_W)), per_seq((rows, C_W)),
            per_seq((B_W, PAGE_SIZE)), per_seq((B_W, PAGE_SIZE)), per_seq((C_W, PAGE_SIZE)), per_seq((C_W, PAGE_SIZE)),
            pl.BlockSpec((4, C_QK), lambda b, g, pt: (0, 0))],
        out_specs=[per_seq((8, B_W)), per_seq((8, C_W))],
        scratch_shapes=[pltpu.VMEM((rows, LANES), F32), pltpu.VMEM((rows, LANES), F32), pltpu.VMEM((rows, B_W), F32),
                        pltpu.VMEM((rows, LANES), F32), pltpu.VMEM((rows, LANES), F32), pltpu.VMEM((rows, C_W), F32)],
    )
    pages = [c for c in caches for _ in range(group)]
    return pl.pallas_call(
        functools.partial(_sample_attn_kernel, group, n_groups, t_new, lam_init),
        grid_spec=grid_spec,
        out_shape=[jax.ShapeDtypeStruct((bd, 8, B_W), F32), jax.ShapeDtypeStruct((bd, 8, C_W), F32)],
        compiler_params=pltpu.CompilerParams(dimension_semantics=("arbitrary", "arbitrary"),
                                             vmem_limit_bytes=VMEM_LIMIT),
        name=f"sample_attn_l{layer}",
    )(page_table, *pages, keys, keys, thr, qd, qc, *new_kv, dl)


def _outproj_kernel(c_scale, x_ref, oa_ref, ob_ref, oc_ref, gate_ref, w_ref, ma_ref, mc_ref, na_ref, nc_ref, y_ref):
    ga = _head_rms(oa_ref[0], ma_ref, A_DK, na_ref[...]) * gate_ref[0, :, 0:A_W]
    gb = ob_ref[0] * gate_ref[0, :, A_W:A_W + B_W]
    gc = _head_rms(oc_ref[0], mc_ref, C_DV, nc_ref[...]) * c_scale * gate_ref[0, :, A_W + B_W:]
    y = x_ref[0] + _dot(ga.astype(BF16), w_ref[0:A_W, :])
    y = y + _dot(gb.astype(BF16), w_ref[A_W:A_W + B_W, :])
    y_ref[0] = y + _dot(gc.astype(BF16), w_ref[A_W + B_W:, :])


def _outproj(layer, x, oa, ob, oc, gate, w_out, onorm_w, subln_w, tm):
    bn, t, d = x.shape
    lam_init = 0.8 - 0.6 * math.exp(-0.3 * layer)

    def tok(width):
        return pl.BlockSpec((1, tm, width), lambda b, i: (b, i, 0))

    def const(shape):
        return pl.BlockSpec(shape, lambda b, i: (0, 0))

    return pl.pallas_call(
        functools.partial(_outproj_kernel, 1.0 - lam_init),
        grid=(bn, t // tm),
        in_specs=[tok(d), tok(A_W), tok(B_W), tok(C_W), tok(d), const((d, d)), const((A_W, A_W)), const((C_W, C_W)),
                  const((1, A_W)), const((1, C_W))],
        out_specs=tok(d),
        out_shape=jax.ShapeDtypeStruct((bn, t, d), F32),
        compiler_params=pltpu.CompilerParams(dimension_semantics=("arbitrary", "arbitrary"),
                                             vmem_limit_bytes=VMEM_LIMIT),
        name=f"outproj_l{layer}_t{t}",
    )(x, oa, ob, oc, gate, w_out.astype(BF16), _block_diag_ones(A_W, A_DK), _block_diag_ones(C_W, C_DV),
      jnp.tile(onorm_w, A_HEADS).reshape(1, A_W), jnp.tile(subln_w, C_HEADS).reshape(1, C_W))


def kernel(x_prompt, x_sample, cache_dsa_k, cache_dsa_v, cache_idx_k, cache_diff_k, cache_diff_v, state_hgrn,
           page_table, norm_w, w_in, w_out, hgrn_lb_logits, hgrn_onorm_w, dsa_qnorm_w, dsa_knorm_w,
           diff_qnorm_w, diff_knorm_w, diff_lambda, diff_subln_w):
    depth = norm_w.shape[0]
    bn, s, d = x_prompt.shape
    bd, t_new, _ = x_sample.shape
    n_pool = cache_dsa_k.shape[1]
    n_pages = page_table.shape[1]
    past = n_pages * PAGE_SIZE
    group = math.gcd(SAMPLE_PAGE_GROUP, n_pages)
    pos_p = jnp.arange(s, dtype=I32)
    pos_s = past + jnp.arange(bd * t_new, dtype=I32) % t_new
    caches = [_token_minor(c) for c in (cache_dsa_k, cache_dsa_v, cache_diff_k, cache_diff_v)]
    idx_cache = _token_minor(cache_idx_k)
    xp = x_prompt
    xs = x_sample.reshape(1, bd * t_new, d)
    outs = {k: [] for k in ('p_bk', 'p_bv', 'p_ik', 'p_ck', 'p_cv', 'p_st', 's_bk', 's_bv', 's_ik', 's_ck', 's_cv', 's_st')}

    def pad_rows(a, n):
        return jnp.pad(a, ((0, 0), (0, n - a.shape[1]), (0, 0)))

    def tok_major(a_t, *feat):
        nf = len(feat)
        a = a_t.reshape((a_t.shape[0],) + feat + (a_t.shape[2],))
        return jnp.transpose(a, (0, nf + 1) + tuple(range(1, nf + 1)))

    def new_page(a_t):
        a = jnp.transpose(a_t[0].reshape(a_t.shape[1], bd, t_new), (1, 0, 2)).astype(BF16)
        return jnp.pad(a, ((0, 0), (0, 0), (0, PAGE_SIZE - t_new)))

    prompt_kv = None
    for l in range(depth):
        w_pack = _pack_w_in(w_in[l])
        norms = (dsa_qnorm_w[l], dsa_knorm_w[l], diff_qnorm_w[l], diff_knorm_w[l])

        o = _inproj(l, xp, pos_p, norm_w[l], w_pack, hgrn_lb_logits, *norms, tm=PROMPT_TILE, depth=depth,
                    stacked=prompt_kv)
        prompt_kv = {n: o[n] for n in _STACKED_OUTPUTS}
        oa, st = _hgrn(o['hg'], jnp.zeros((bn, A_HEADS, A_DK, A_DK), F32), PROMPT_TILE, f"hgrn_prompt_l{l}")
        ob = _dsa_prompt(o, PROMPT_TILE)
        oc = _diff_prompt(l, o, diff_lambda[l], PROMPT_TILE)
        xp = _outproj(l, xp, oa, ob, oc, o['gate'], w_out[l], hgrn_onorm_w[l], diff_subln_w[l], min(512, s))
        outs['p_st'].append(st)

        o = _inproj(l, xs, pos_s, norm_w[l], w_pack, hgrn_lb_logits, *norms, tm=bd * t_new)
        per_seq = lambda a: a.reshape(bd, t_new, a.shape[-1])
        hg = pad_rows(per_seq(o['hg']), HGRN_CHUNK)
        oa, st = _hgrn(hg, state_hgrn[l], HGRN_CHUNK, f"hgrn_sample_l{l}")
        oa = oa[:, :t_new].reshape(1, bd * t_new, A_W)
        iq_rows = o['iqb'].reshape(bd, t_new * IDX_HEADS, IDX_D)
        w_rows = jnp.broadcast_to(jnp.swapaxes(o['iwt'][0], 0, 1).reshape(bd, t_new * IDX_HEADS, 1),
                                  (bd, t_new * IDX_HEADS, LANES))
        keys, thr = _sample_index(l, idx_cache, page_table, iq_rows, w_rows, new_page(o['ikt']),
                                  math.gcd(SAMPLE_INDEX_GROUP, n_pages), min(TOPK_MAX, (past + t_new) // 4))
        lane = jnp.arange(B_W)
        r8 = jnp.arange(8)
        d_mask = (lane[None, :] // B_HD == r8[:, None])
        c_mask = (lane[None, :] // C_QK == r8[:, None])
        qd = jnp.where(d_mask[None, None], per_seq(o['bqb'])[:, :, None, :], 0).reshape(bd, 8 * t_new, B_W)
        qc = jnp.where(c_mask[None, None], per_seq(o['cqb'])[:, :, None, :], 0).reshape(bd, 8 * t_new, C_W)
        new_kv = [new_page(o[k]) for k in ('bkt', 'bvt32', 'ckt', 'cvt32')]
        ob, oc = _sample_attn(l, caches, page_table, keys, thr, qd, qc, new_kv, diff_lambda[l], group)
        ob = ob[:, :t_new].reshape(1, bd * t_new, B_W)
        oc = oc[:, :t_new].reshape(1, bd * t_new, C_W)
        xs = _outproj(l, xs, oa, ob, oc, o['gate'], w_out[l], hgrn_onorm_w[l], diff_subln_w[l], bd * t_new)
        outs['s_bk'].append(tok_major(o['bkt'], B_HEADS, B_HD).reshape(bd, t_new, B_HEADS, B_HD))
        outs['s_bv'].append(tok_major(o['bvt32'], B_HEADS, B_HD).reshape(bd, t_new, B_HEADS, B_HD))
        outs['s_ik'].append(tok_major(o['ikt'], IDX_D).reshape(bd, t_new, IDX_D))
        outs['s_ck'].append(tok_major(o['ckt'], C_HEADS, 2, C_QK).reshape(bd, t_new, C_HEADS, 2, C_QK))
        outs['s_cv'].append(tok_major(o['cvt32'], C_HEADS, C_DV).reshape(bd, t_new, C_HEADS, C_DV))
        outs['s_st'].append(st)

    stk = {k: jnp.stack(v) for k, v in outs.items() if v}

    def layers_tok_major(name, *feat):
        a = prompt_kv[name]
        return tok_major(a.reshape(depth * bn, a.shape[2], s), *feat).reshape((depth, bn, s) + feat)

    return (xp, xs.reshape(bd, t_new, d),
            layers_tok_major('bkt', B_HEADS, B_HD), layers_tok_major('bvt32', B_HEADS, B_HD),
            layers_tok_major('ikt', IDX_D), layers_tok_major('ckt', C_HEADS, 2, C_QK),
            layers_tok_major('cvt32', C_HEADS, C_DV), stk['p_st'],
            stk['s_bk'], stk['s_bv'], stk['s_ik'], stk['s_ck'], stk['s_cv'], stk['s_st'])
```

```python
import functools
import math

import numpy as np
import jax
import jax.numpy as jnp
from jax import lax
from jax.experimental import pallas as pl
from jax.experimental.pallas import tpu as pltpu

F32 = jnp.float32
BF16 = jnp.bfloat16
I32 = jnp.int32
I16 = jnp.int16

D_MODEL = 1024
A_W, A_HEADS, A_DK = 256, 4, 64
B_W, B_HEADS, B_HD = 384, 6, 64
IDX_HEADS, IDX_D = 8, 64
C_W, C_HEADS, C_DV, C_QK = 384, 4, 96, 48
TOPK_MAX = 256
PAGE_SIZE = 128
ROPE_THETA = 500000.0
EPS = 1e-6
NEG_BIG = -1e30
LOG2E = math.log2(math.e)
HGRN_CHUNK = 64
LANES = 128
INT_MIN = -(2 ** 31)
VMEM_LIMIT = 56 * 1024 * 1024
PROMPT_TILE = 256
SAMPLE_PAGE_GROUP = 16
SAMPLE_INDEX_GROUP = 32
SAMPLE_SEQS_PER_STEP = 2
HGRN_SEQS_PER_STEP = 2

_OFF_A = 0
_OFF_BQ, _OFF_BK, _OFF_BV, _OFF_BG = 1024, 1408, 1792, 2176
_OFF_IQ, _OFF_IK, _OFF_IW = 2560, 3072, 3200
_OFF_CQ, _OFF_CK, _OFF_CV, _OFF_CG = 3328, 3712, 4096, 4480
_N_PACK = 4864


def _dot(a, b):
    return jnp.dot(a, b, preferred_element_type=F32)


def _dot_nt(a, b):
    return lax.dot_general(a, b, (((1,), (1,)), ((), ())), preferred_element_type=F32)


def _pack_w_in(w):
    i_k = w[:, 3072:3136]
    i_w = w[:, 3136:3144]
    pad = jnp.zeros((w.shape[0], LANES - IDX_HEADS), w.dtype)
    packed = jnp.concatenate([w[:, :3072], i_k, i_k, i_w, pad, w[:, 3144:]], axis=1).astype(BF16)
    wv_t = jnp.concatenate([w[:, 1792:2176], w[:, 3912:4296]], axis=1).T.astype(BF16)
    return packed, wv_t


def _rope_tables(pos, d, n_rep):
    rot = d // 4
    half = rot // 2
    inv = jnp.power(ROPE_THETA, -2.0 * jnp.arange(half, dtype=F32) / rot)
    ang = pos.astype(F32)[:, None] * inv[None, :]
    cos, sin = jnp.cos(ang), jnp.sin(ang)
    t = pos.shape[0]
    one = jnp.ones((t, d - rot), F32)
    zero = jnp.zeros((t, d - half), F32)
    c = jnp.concatenate([cos, cos, one], axis=1)
    sa = jnp.concatenate([-sin, zero], axis=1)
    sb = jnp.concatenate([jnp.zeros((t, half), F32), sin, jnp.zeros((t, d - rot), F32)], axis=1)
    return jnp.stack([jnp.tile(c, (1, n_rep)), jnp.tile(sa, (1, n_rep)), jnp.tile(sb, (1, n_rep))])


def _block_diag_ones(width, group):
    g = np.arange(width) // group
    return jnp.asarray((g[:, None] == g[None, :]).astype(np.float32), dtype=BF16)


def _pad_perm():
    p = np.zeros((C_W, C_HEADS * LANES), np.float32)
    for j in range(C_W):
        p[j, (j // C_DV) * LANES + (j % C_DV)] = 1.0
    return jnp.asarray(p, dtype=BF16)


def _rope(x, tab_ref, width, half):
    c = tab_ref[0, :, :width]
    sa = tab_ref[1, :, :width]
    sb = tab_ref[2, :, :width]
    return x * c + pltpu.roll(x, width - half, 1) * sa + pltpu.roll(x, half, 1) * sb


def _head_rms(x, m_ref, group, w_row):
    ms = _dot((x * x).astype(BF16), m_ref[...]) * (1.0 / group)
    return x * lax.rsqrt(ms + EPS) * w_row


def _inproj_kernel(layer, n_carried, x_ref, nw_ref, w_ref, wvt_ref, lbl_ref, t64_ref, t48_ref, m64_ref, m48_ref,
                   ppad_ref, bqn_ref, bkn_ref, cqn_ref, ckn_ref, *rest):
    (hg_ref, gate_ref, bkt_ref, bvt32_ref, ikt_ref, ckt_ref, cvt32_ref,
     bqb_ref, bkb_ref, bvt_ref, iqb_ref, ikd_ref, iwt_ref, cqb_ref, cqp_ref, ckp_ref, cvt_ref) = rest[n_carried:]
    xf = x_ref[0]
    y = xf * lax.rsqrt(jnp.mean(xf * xf, axis=-1, keepdims=True) + EPS)
    h = (y * nw_ref[...]).astype(BF16)

    def seg(a, b):
        return _dot(h, w_ref[:, a:b])

    lg = lbl_ref[...]
    e = jnp.exp(lg - jnp.max(lg, axis=0, keepdims=True))
    soft = e / jnp.sum(e, axis=0, keepdims=True)
    lb = jnp.zeros((1, A_W), F32)
    for i in range(1, layer + 1):
        lb = lb + soft[i:i + 1, :]
    fz = seg(_OFF_A + 256, _OFF_A + 512)
    en = jnp.exp(-jnp.abs(fz))
    r = 1.0 / (1.0 + en)
    pos_side = fz >= 0
    sig_p = jnp.where(pos_side, r, en * r)
    sig_n = jnp.where(pos_side, en * r, r)
    hg_ref[0, :, 0:256] = seg(_OFF_A, _OFF_A + 256)
    hg_ref[0, :, 256:512] = (1.0 - lb) * sig_n
    hg_ref[0, :, 512:768] = jnp.log(lb + (1.0 - lb) * sig_p)
    hg_ref[0, :, 768:1024] = seg(_OFF_A + 512, _OFF_A + 768)

    def silu(g):
        return g / (1.0 + jnp.exp(-g))

    gate_ref[0, :, 0:256] = silu(seg(_OFF_A + 768, _OFF_A + 1024))
    gate_ref[0, :, 256:640] = silu(seg(_OFF_BG, _OFF_BG + B_W))
    gate_ref[0, :, 640:1024] = silu(seg(_OFF_CG, _OFF_CG + C_W))

    bq = _rope(_head_rms(seg(_OFF_BQ, _OFF_BQ + B_W), m64_ref, B_HD, bqn_ref[...]), t64_ref, B_W, B_HD // 8)
    bqb_ref[0] = (bq * (B_HD ** -0.5 * LOG2E)).astype(BF16)
    bk = _rope(_head_rms(seg(_OFF_BK, _OFF_BK + B_W), m64_ref, B_HD, bkn_ref[...]), t64_ref, B_W, B_HD // 8)
    bkt_ref[...] = bk.T
    bkb_ref[0] = bk.astype(BF16)
    bvt = _dot_nt(wvt_ref[0:B_W, :], h)
    bvt32_ref[...] = bvt
    bvt_ref[0, 0] = bvt.astype(BF16)

    iq = _rope(seg(_OFF_IQ, _OFF_IQ + IDX_HEADS * IDX_D), t64_ref, IDX_HEADS * IDX_D, IDX_D // 8)
    iqb_ref[0] = (iq * (IDX_D ** -0.5)).astype(BF16)
    ikd = _rope(seg(_OFF_IK, _OFF_IK + LANES), t64_ref, LANES, IDX_D // 8)
    ikt_ref[...] = ikd.T[0:IDX_D, :]
    ikd_ref[0] = ikd.astype(BF16)
    iw = seg(_OFF_IW, _OFF_IW + LANES) * (IDX_HEADS ** -0.5)
    iwt_ref[0] = iw.T[0:IDX_HEADS, :]

    cq = _rope(_head_rms(seg(_OFF_CQ, _OFF_CQ + C_W), m48_ref, C_QK, cqn_ref[...]), t48_ref, C_W, C_QK // 8)
    cqb = (cq * (C_QK ** -0.5 * LOG2E)).astype(BF16)
    cqb_ref[0] = cqb
    cqp_ref[0] = _dot(cqb, ppad_ref[...]).astype(BF16)
    ck = _rope(_head_rms(seg(_OFF_CK, _OFF_CK + C_W), m48_ref, C_QK, ckn_ref[...]), t48_ref, C_W, C_QK // 8)
    ckt_ref[...] = ck.T
    ckp_ref[0] = _dot(ck.astype(BF16), ppad_ref[...]).astype(BF16)
    cvt = _dot_nt(wvt_ref[B_W:B_W + C_W, :], h)
    cvt32_ref[...] = cvt
    cvt_ref[0, 0] = cvt.astype(BF16)


_STACKED_OUTPUTS = ('bkt', 'bvt32', 'ikt', 'ckt', 'cvt32')


def _inproj(layer, x, pos, norm_w, w_packs, lb_logits, bqn, bkn, cqn, ckn, tm, depth=None, stacked=None):
    w_pack, wv_t = w_packs
    bn, t, d = x.shape
    nt = t // tm
    t64 = _rope_tables(pos, IDX_D, IDX_HEADS)
    t48 = _rope_tables(pos, C_QK, 2 * C_HEADS)
    m64 = _block_diag_ones(B_W, B_HD)
    m48 = _block_diag_ones(C_W, C_QK)
    ppad = _pad_perm()

    def tok(width, dtype):
        return jax.ShapeDtypeStruct((bn, t, width), dtype), pl.BlockSpec((1, tm, width), lambda i, b: (b, i, 0))

    def trf(rows):
        if depth is None:
            return (jax.ShapeDtypeStruct((bn, rows, t), F32), pl.BlockSpec((None, rows, tm), lambda i, b: (b, 0, i)))
        return (jax.ShapeDtypeStruct((depth, bn, rows, t), F32),
                pl.BlockSpec((None, None, rows, tm), lambda i, b: (layer, b, 0, i)))

    def tr(rows, dtype):
        return (jax.ShapeDtypeStruct((bn, nt, rows, tm), dtype),
                pl.BlockSpec((1, 1, rows, tm), lambda i, b: (b, i, 0, 0)))

    outs = dict(
        hg=tok(1024, F32), gate=tok(1024, F32), bkt=trf(B_W), bvt32=trf(B_W), ikt=trf(IDX_D), ckt=trf(C_W), cvt32=trf(C_W),
        bqb=tok(B_W, BF16), bkb=tok(B_W, BF16), bvt=tr(B_W, BF16), iqb=tok(IDX_HEADS * IDX_D, BF16),
        ikd=tok(LANES, BF16),
        iwt=(jax.ShapeDtypeStruct((bn, IDX_HEADS, t), F32), pl.BlockSpec((1, IDX_HEADS, tm), lambda i, b: (b, 0, i))),
        cqb=tok(C_W, BF16), cqp=tok(C_HEADS * LANES, BF16), ckp=tok(C_HEADS * LANES, BF16), cvt=tr(C_W, BF16),
    )
    names = list(outs)

    def const(shape):
        nd = len(shape)
        return pl.BlockSpec(shape, lambda i, b: (0,) * nd)

    in_specs = [
        pl.BlockSpec((1, tm, d), lambda i, b: (b, i, 0)),
        const((1, d)),
        const((d, _N_PACK)),
        const((B_W + C_W, d)),
        const(lb_logits.shape),
        pl.BlockSpec((3, tm, IDX_HEADS * IDX_D), lambda i, b: (0, i, 0)),
        pl.BlockSpec((3, tm, C_W), lambda i, b: (0, i, 0)),
        const((B_W, B_W)), const((C_W, C_W)), const((C_W, C_HEADS * LANES)),
        const((1, B_W)), const((1, B_W)), const((1, C_W)), const((1, C_W)),
    ]
    carried = [] if stacked is None else [stacked[n] for n in _STACKED_OUTPUTS]
    aliases = {len(in_specs) + j: names.index(n) for j, n in enumerate(_STACKED_OUTPUTS)} if carried else {}
    res = pl.pallas_call(
        functools.partial(_inproj_kernel, layer, len(carried)),
        grid=(nt, bn),
        in_specs=in_specs + [pl.BlockSpec(memory_space=pl.ANY)] * len(carried),
        out_specs=[outs[n][1] for n in names],
        out_shape=[outs[n][0] for n in names],
        input_output_aliases=aliases,
        compiler_params=pltpu.CompilerParams(dimension_semantics=("arbitrary", "arbitrary"),
                                             vmem_limit_bytes=VMEM_LIMIT),
        name=f"inproj_l{layer}_t{t}",
    )(x, norm_w.reshape(1, d), w_pack, wv_t, lb_logits, t64, t48, m64, m48, ppad,
      jnp.tile(bqn, B_HEADS).reshape(1, B_W), jnp.tile(bkn, B_HEADS).reshape(1, B_W),
      jnp.tile(cqn, 2 * C_HEADS).reshape(1, C_W), jnp.tile(ckn, 2 * C_HEADS).reshape(1, C_W), *carried)
    return dict(zip(names, res))


_HGRN_LEVELS = (1, 2, 4, 8, 16, 32)


def _hgrn_tables():
    c = HGRN_CHUNK
    t = np.arange(c)[:, None]
    u = np.arange(c)[None, :]
    mats = [(u <= t), (u > t)]
    masks = [(t == u)]
    for h in _HGRN_LEVELS:
        r = (t // (2 * h)) * (2 * h) + h - 1
        right = (t % (2 * h)) >= h
        mats.append(np.where(right, (u > r) & (u <= t), (u > t) & (u <= r)))
        masks.append((t // (2 * h) == u // (2 * h)) & right & ((u % (2 * h)) < h))
    tall = np.tile(np.concatenate(mats, axis=0).astype(np.float32), (1, 3))
    mk = np.stack([np.tile(m, (1, 2)) for m in masks]).astype(np.float32)
    return jnp.asarray(tall, dtype=BF16), jnp.asarray(mk)


def _pair_expand(a):
    lane = lax.broadcasted_iota(I32, a.shape, 1)
    zero = jnp.zeros_like(a)
    return jnp.concatenate([jnp.where(lane < A_DK, a, zero), jnp.where(lane >= A_DK, a, zero)], axis=0)


def _hgrn_kernel(nchunk, nseq, hg_ref, tall_ref, mk_ref, s0_ref, o_ref, sf_ref, st_ref):
    c = HGRN_CHUNK

    @pl.when(pl.program_id(1) == 0)
    def _():
        st_ref[...] = s0_ref[...]

    row = lax.broadcasted_iota(I32, (2 * A_DK, 2 * A_DK), 0)
    col = lax.broadcasted_iota(I32, (2 * A_DK, 2 * A_DK), 1)
    same_head = (row // A_DK) == (col // A_DK)
    zpad = jnp.zeros((c, 2 * A_DK), BF16)

    units = [(sq, p) for sq in range(nseq) for p in range(2)]
    n_lev = len(_HGRN_LEVELS)
    for ci in range(nchunk):
        rows = slice(ci * c, (ci + 1) * c)
        exs = []
        for sq in range(nseq):
            lf = hg_ref[sq, rows, 512:768]
            l1 = lf.astype(BF16)
            r1 = lf - l1.astype(F32)
            l2 = r1.astype(BF16)
            l3 = (r1 - l2.astype(F32)).astype(BF16)
            exs.append(_dot(tall_ref[...], jnp.concatenate([l1, l2, l3], axis=0)))
        exs = [jnp.exp(e) for e in exs]
        inter, level, upd, vexp, states = [], [], [], [], []
        for sq, p in units:
            ex = exs[sq]
            lanes = slice(p * 128, (p + 1) * 128)
            qp = hg_ref[sq, rows, lanes]
            kp = hg_ref[sq, rows, 256 + p * 128:256 + (p + 1) * 128]
            vp = hg_ref[sq, rows, 768 + p * 128:768 + (p + 1) * 128].astype(BF16)
            st = st_ref[sq, p]
            states.append(st)
            vexp.append(_pair_expand(vp))
            inter.append(_dot_nt((qp * ex[0:c, lanes]).astype(BF16), st.astype(BF16)))
            lv = [_dot_nt(qp.astype(BF16), _pair_expand(kp.astype(BF16)))]
            for li in range(n_lev):
                xl = ex[(2 + li) * c:(3 + li) * c, lanes]
                lv.append(_dot_nt((qp * xl).astype(BF16), _pair_expand((kp * xl).astype(BF16))))
            level.append(lv)
            kh = (kp * ex[c:2 * c, lanes]).astype(BF16)
            vt = jnp.concatenate([vp, zpad], axis=0).astype(F32).T.astype(BF16)
            upd.append(_dot(vt, jnp.concatenate([kh, zpad], axis=0)))
        scores = []
        for lv in level:
            sc = mk_ref[0] * lv[0]
            for li in range(n_lev):
                sc = sc + mk_ref[1 + li] * lv[1 + li]
            scores.append(sc.astype(BF16))
        intra = [_dot(sc, ve) for sc, ve in zip(scores, vexp)]
        for u, (sq, p) in enumerate(units):
            lanes = slice(p * 128, (p + 1) * 128)
            o_ref[sq, rows, lanes] = inter[u] + intra[u]
            decay = exs[sq][c - 1:c, lanes]
            st_ref[sq, p] = jnp.where(same_head, states[u] * decay + upd[u], 0.0)
    sf_ref[...] = st_ref[...]


def _state_to_pairs(s):
    bn = s.shape[0]
    st = jnp.swapaxes(s, -1, -2).reshape(bn, 2, 2, A_DK, A_DK)
    z = jnp.zeros((bn, 2, A_DK, A_DK), s.dtype)
    top = jnp.concatenate([st[:, :, 0], z], axis=-1)
    bot = jnp.concatenate([z, st[:, :, 1]], axis=-1)
    return jnp.concatenate([top, bot], axis=-2)


def _pairs_to_state(sp):
    bn = sp.shape[0]
    h0 = sp[:, :, :A_DK, :A_DK]
    h1 = sp[:, :, A_DK:, A_DK:]
    return jnp.swapaxes(jnp.stack([h0, h1], axis=2).reshape(bn, A_HEADS, A_DK, A_DK), -1, -2)


def _hgrn(hg, s0, tb, name):
    bn, t, _ = hg.shape
    nseq = HGRN_SEQS_PER_STEP
    assert bn % nseq == 0
    tall, mk = _hgrn_tables()
    o, sf = pl.pallas_call(
        functools.partial(_hgrn_kernel, tb // HGRN_CHUNK, nseq),
        grid=(bn // nseq, t // tb),
        in_specs=[pl.BlockSpec((nseq, tb, 1024), lambda b, j: (b, j, 0)),
                  pl.BlockSpec(tall.shape, lambda b, j: (0, 0)),
                  pl.BlockSpec(mk.shape, lambda b, j: (0, 0, 0)),
                  pl.BlockSpec((nseq, 2, 128, 128), lambda b, j: (b, 0, 0, 0))],
        out_specs=[pl.BlockSpec((nseq, tb, A_W), lambda b, j: (b, j, 0)),
                   pl.BlockSpec((nseq, 2, 128, 128), lambda b, j: (b, 0, 0, 0))],
        out_shape=[jax.ShapeDtypeStruct((bn, t, A_W), F32), jax.ShapeDtypeStruct((bn, 2, 128, 128), F32)],
        scratch_shapes=[pltpu.VMEM((nseq, 2, 128, 128), F32)],
        compiler_params=pltpu.CompilerParams(dimension_semantics=("arbitrary", "arbitrary"),
                                             vmem_limit_bytes=VMEM_LIMIT),
        name=name,
    )(hg, tall, mk, _state_to_pairs(s0))
    return o, _pairs_to_state(sf)


DSA_TQ = 256


def _key_of(score):
    bits = lax.bitcast_convert_type(score, I32)
    key = bits ^ ((bits >> 31) & 0x7FFFFFFF)
    return jnp.where(score == 0.0, 0, key)


def _count_rows(pred_fn, nk, ch, width):
    def body(kc, acc):
        hit = jnp.where(pred_fn(kc), 1, 0)
        return acc + jnp.sum(hit.reshape(ch // 8, 8, width), axis=0)

    acc = lax.fori_loop(0, nk, body, jnp.zeros((8, width), I32))
    return jnp.sum(acc, axis=0, keepdims=True)


def _select_topk(keys_ref, top_ref, nk, ch, width, n_keep, pos_bits):
    def chunk(kc):
        return keys_ref[pl.ds(kc * ch, ch), :]

    def kpos(kc):
        return kc * ch + lax.broadcasted_iota(I32, (ch, width), 0)

    def count16(c16):
        def body(kc, acc):
            x = top_ref[pl.ds(kc * ch, ch), :]
            hit = jnp.where(x >= c16, jnp.ones_like(x), jnp.zeros_like(x)).reshape(ch // 16, 16, width)
            for j in range(ch // 16):
                acc = acc + hit[j]
            return acc

        acc = lax.fori_loop(0, nk, body, jnp.zeros((16, width), I16))
        return jnp.sum(acc.astype(I32), axis=0, keepdims=True)

    def top_step(i, tu):
        cand = tu | jnp.left_shift(jnp.int32(1), 31 - i)
        cnt = count16(((cand ^ INT_MIN) >> 16).astype(I16))
        return jnp.where(cnt >= n_keep, cand, tu)

    tu = lax.fori_loop(0, 16, top_step, jnp.zeros((1, width), I32))

    h16 = (tu ^ INT_MIN) >> 16

    def bake(kc, carry):
        k = chunk(kc)
        t = k >> 16
        low = (k & 0xFFFF) - 32768
        top_ref[pl.ds(kc * ch, ch), :] = jnp.where(t > h16, 32767, jnp.where(t == h16, low, -32768)).astype(I16)
        return carry

    lax.fori_loop(0, nk, bake, 0)

    def low_step(i, tu):
        cand = tu | jnp.left_shift(jnp.int32(1), 31 - i)
        cnt = count16(((cand & 0xFFFF) - 32768).astype(I16))
        return jnp.where(cnt >= n_keep, cand, tu)

    thr = lax.fori_loop(16, 32, low_step, tu) ^ INT_MIN
    cnt_ge = _count_rows(lambda kc: chunk(kc) >= thr, nk, ch, width)
    excess = jnp.where(thr > INT_MIN, cnt_ge - n_keep, 0)

    @pl.when(jnp.max(excess) > 0)
    def _():
        need = n_keep - _count_rows(lambda kc: chunk(kc) > thr, nk, ch, width)

        def pos_step(i, c):
            cand = c | jnp.left_shift(jnp.int32(1), pos_bits - 1 - i)
            before = _count_rows(lambda kc: (chunk(kc) == thr) & (kpos(kc) < cand), nk, ch, width)
            return jnp.where(before <= need - 1, cand, c)

        cut = lax.fori_loop(0, pos_bits, pos_step, jnp.zeros((1, width), I32))

        def drop(kc, carry):
            k = chunk(kc)
            late_tie = (k == thr) & (kpos(kc) > cut) & (excess > 0)
            keys_ref[pl.ds(kc * ch, ch), :] = jnp.where(late_tie, INT_MIN, k)
            return carry

        lax.fori_loop(0, nk, drop, 0)

    return jnp.maximum(thr, INT_MIN + 1)


def _lane_halves(x):
    lane = lax.broadcasted_iota(I32, x.shape, 1)
    zero = jnp.zeros_like(x)
    return jnp.concatenate([jnp.where(lane < B_HD, x, zero), jnp.where(lane >= B_HD, x, zero)], axis=0)


def _softmax_updates_t(scores, values_t, m_ref, l_ref, acc_ref):
    alphas, probs = [], []
    for j, s in enumerate(scores):
        m_old = m_ref[j, 0:1, :]
        m_new = jnp.maximum(m_old, jnp.max(s, axis=0, keepdims=True))
        alpha = jnp.exp2(m_old - m_new)
        e = jnp.exp2(s - m_new)
        l_ref[j, 0:1, :] = alpha * l_ref[j, 0:1, :] + jnp.sum(e, axis=0, keepdims=True)
        m_ref[j, 0:1, :] = m_new
        alphas.append(alpha)
        probs.append(e.astype(BF16))
    updates = [_dot(v_t, p) for v_t, p in zip(values_t, probs)]
    for j, (alpha, upd) in enumerate(zip(alphas, updates)):
        acc_ref[j] = alpha * acc_ref[j] + upd


def _dsa_prompt_kernel(n_keep, ch, pos_bits, iqb_ref, iwt_ref, ikd_ref, bqb_ref, bkb_ref, bvt_ref, o_ref,
                       keys_ref, top_ref, m_ref, l_ref, acc_ref):
    tq = DSA_TQ
    qb = pl.program_id(1)
    nk = (qb * tq + tq + ch - 1) // ch
    qpos = qb * tq + lax.broadcasted_iota(I32, (1, tq), 1)

    iq = iqb_ref[0]
    iq_rows = jnp.concatenate([_lane_halves(iq[:, j * LANES:(j + 1) * LANES]) for j in range(IDX_HEADS // 2)], axis=0)
    wt = iwt_ref[0]

    def index_chunk(kc, carry):
        hc = ch // 2
        raw = [_dot_nt(ikd_ref[0, pl.ds(kc * ch + i * hc, hc), :], iq_rows) for i in range(2)]
        for i, s in enumerate(raw):
            acc = jnp.zeros((hc, tq), F32)
            for h in range(IDX_HEADS):
                acc = acc + wt[h:h + 1, :] * jnp.maximum(s[:, h * tq:(h + 1) * tq], 0.0)
            kp = kc * ch + i * hc + lax.broadcasted_iota(I32, (hc, tq), 0)
            key = jnp.where(kp <= qpos, _key_of(acc), INT_MIN)
            keys_ref[pl.ds(kc * ch + i * hc, hc), :] = key
            top_ref[pl.ds(kc * ch + i * hc, hc), :] = (key >> 16).astype(I16)
        return carry

    lax.fori_loop(0, nk, index_chunk, 0)

    @pl.when(nk % 2 == 1)
    def _():
        keys_ref[pl.ds(nk * ch, ch), :] = jnp.full((ch, tq), INT_MIN, I32)
        top_ref[pl.ds(nk * ch, ch), :] = jnp.full((ch, tq), INT_MIN >> 16, I16)

    thr = _select_topk(keys_ref, top_ref, (nk + 1) // 2, 2 * ch, tq, n_keep, pos_bits)

    n_pairs = B_HEADS // 2
    bq = bqb_ref[0]
    q_rows = [_lane_halves(bq[:, p * LANES:(p + 1) * LANES]) for p in range(n_pairs)]
    m_ref[...] = jnp.full(m_ref.shape, NEG_BIG, F32)
    l_ref[...] = jnp.zeros(l_ref.shape, F32)
    acc_ref[...] = jnp.zeros(acc_ref.shape, F32)

    def attend(kc, carry):
        sel = keys_ref[pl.ds(kc * ch, ch), :] >= thr
        raw = [_dot_nt(bkb_ref[0, pl.ds(kc * ch, ch), p * LANES:(p + 1) * LANES], q_rows[p])
               for p in range(n_pairs)]
        scores = [jnp.concatenate([jnp.where(sel, s[:, :tq], NEG_BIG), jnp.where(sel, s[:, tq:], NEG_BIG)], axis=1)
                  for s in raw]
        values_t = [bvt_ref[0, kc, p * LANES:(p + 1) * LANES, :] for p in range(n_pairs)]
        _softmax_updates_t(scores, values_t, m_ref, l_ref, acc_ref)
        return carry

    lax.fori_loop(0, nk, attend, 0)
    outs = []
    for p in range(n_pairs):
        inv = 1.0 / l_ref[p, 0:1, :]
        outs.append(acc_ref[p, 0:B_HD, 0:tq] * inv[:, 0:tq])
        outs.append(acc_ref[p, B_HD:, tq:] * inv[:, tq:])
    o_ref[0] = jnp.concatenate(outs, axis=0).T


def _dsa_prompt(o, ch):
    bn, s, _ = o['bkb'].shape
    n_keep = min(TOPK_MAX, s // 4)
    nb = s // DSA_TQ
    assert (s // ch) % 2 == 0, "the threshold search walks key chunks in pairs"
    n_pairs = B_HEADS // 2
    return pl.pallas_call(
        functools.partial(_dsa_prompt_kernel, n_keep, ch, s.bit_length()),
        grid=(bn, nb),
        in_specs=[pl.BlockSpec((1, DSA_TQ, IDX_HEADS * IDX_D), lambda b, q: (b, q, 0)),
                  pl.BlockSpec((1, IDX_HEADS, DSA_TQ), lambda b, q: (b, 0, q)),
                  pl.BlockSpec((1, s, LANES), lambda b, q: (b, 0, 0)),
                  pl.BlockSpec((1, DSA_TQ, B_W), lambda b, q: (b, q, 0)),
                  pl.BlockSpec((1, s, B_W), lambda b, q: (b, 0, 0)),
                  pl.BlockSpec((1, s // ch, B_W, ch), lambda b, q: (b, 0, 0, 0))],
        out_specs=pl.BlockSpec((1, DSA_TQ, B_W), lambda b, q: (b, q, 0)),
        out_shape=jax.ShapeDtypeStruct((bn, s, B_W), F32),
        scratch_shapes=[pltpu.VMEM((s, DSA_TQ), I32), pltpu.VMEM((s, DSA_TQ), I16),
                        pltpu.VMEM((n_pairs, 8, 2 * DSA_TQ), F32),
                        pltpu.VMEM((n_pairs, 8, 2 * DSA_TQ), F32), pltpu.VMEM((n_pairs, LANES, 2 * DSA_TQ), F32)],
        compiler_params=pltpu.CompilerParams(dimension_semantics=("arbitrary", "arbitrary"),
                                             vmem_limit_bytes=VMEM_LIMIT),
        name="dsa_prompt",
    )(o['iqb'], o['iwt'], o['ikd'], o['bqb'], o['bkb'], o['bvt'])


def _diff_lambda(dl, lam_init):
    a = jnp.sum(dl[0:1, :] * dl[1:2, :], axis=1, keepdims=True)
    b = jnp.sum(dl[2:3, :] * dl[3:4, :], axis=1, keepdims=True)
    return jnp.exp(a) - jnp.exp(b) + lam_init


def _diff_prompt_kernel(lam_init, ch, dl_ref, cqp_ref, ckp_ref, cvt_ref, o_ref, m_ref, l_ref, acc_ref):
    tq = DSA_TQ
    qb = pl.program_id(1)
    nk = (qb * tq + tq + ch - 1) // ch
    n_full = (qb * tq) // ch
    qpos = qb * tq + lax.broadcasted_iota(I32, (1, tq), 1)
    lam = _diff_lambda(dl_ref[...], lam_init)
    cq = cqp_ref[0]
    lane = lax.broadcasted_iota(I32, (tq, LANES), 1)
    q_rows = []
    for h in range(C_HEADS):
        qh = cq[:, h * LANES:(h + 1) * LANES]
        zero = jnp.zeros_like(qh)
        q_rows.append(jnp.concatenate([jnp.where(lane < C_QK, qh, zero),
                                       jnp.where((lane >= C_QK) & (lane < 2 * C_QK), qh, zero)], axis=0))
    m_ref[...] = jnp.full(m_ref.shape, NEG_BIG, F32)
    l_ref[...] = jnp.zeros(l_ref.shape, F32)
    acc_ref[...] = jnp.zeros(acc_ref.shape, F32)

    def attend(kc, masked):
        scores = [_dot_nt(ckp_ref[0, pl.ds(kc * ch, ch), h * LANES:(h + 1) * LANES], q_rows[h])
                  for h in range(C_HEADS)]
        if masked:
            vis = (kc * ch + lax.broadcasted_iota(I32, (ch, tq), 0)) <= qpos
            scores = [jnp.concatenate([jnp.where(vis, s[:, :tq], NEG_BIG), jnp.where(vis, s[:, tq:], NEG_BIG)],
                                      axis=1) for s in scores]
        values_t = [cvt_ref[0, kc, h * C_DV:(h + 1) * C_DV, :] for h in range(C_HEADS)]
        _softmax_updates_t(scores, values_t, m_ref, l_ref, acc_ref)

    def full_chunk(kc, carry):
        attend(kc, False)
        return carry

    def diag_chunk(kc, carry):
        attend(kc, True)
        return carry

    lax.fori_loop(0, n_full, full_chunk, 0)
    lax.fori_loop(n_full, nk, diag_chunk, 0)
    outs = []
    for h in range(C_HEADS):
        inv = 1.0 / l_ref[h, 0:1, :]
        acc = acc_ref[h]
        outs.append(acc[:, :tq] * inv[:, :tq] - lam * (acc[:, tq:] * inv[:, tq:]))
    o_ref[0] = jnp.concatenate(outs, axis=0).T


def _diff_prompt(layer, o, dl, ch):
    bn, s, _ = o['ckp'].shape
    lam_init = 0.8 - 0.6 * math.exp(-0.3 * layer)
    return pl.pallas_call(
        functools.partial(_diff_prompt_kernel, lam_init, ch),
        grid=(bn, s // DSA_TQ),
        in_specs=[pl.BlockSpec((4, C_QK), lambda b, q: (0, 0)),
                  pl.BlockSpec((1, DSA_TQ, C_HEADS * LANES), lambda b, q: (b, q, 0)),
                  pl.BlockSpec((1, s, C_HEADS * LANES), lambda b, q: (b, 0, 0)),
                  pl.BlockSpec((1, s // ch, C_W, ch), lambda b, q: (b, 0, 0, 0))],
        out_specs=pl.BlockSpec((1, DSA_TQ, C_W), lambda b, q: (b, q, 0)),
        out_shape=jax.ShapeDtypeStruct((bn, s, C_W), F32),
        scratch_shapes=[pltpu.VMEM((C_HEADS, 8, 2 * DSA_TQ), F32), pltpu.VMEM((C_HEADS, 8, 2 * DSA_TQ), F32),
                        pltpu.VMEM((C_HEADS, C_DV, 2 * DSA_TQ), F32)],
        compiler_params=pltpu.CompilerParams(dimension_semantics=("arbitrary", "arbitrary"),
                                             vmem_limit_bytes=VMEM_LIMIT),
        name="diff_prompt",
    )(dl, o['cqp'], o['ckp'], o['cvt'])


def _token_minor(cache):
    nd = cache.ndim
    c = jnp.transpose(cache, (0, 1) + tuple(range(3, nd)) + (2,))
    return c.reshape(c.shape[0], c.shape[1], -1, c.shape[-1])


def _page_specs(layer, n, group, width):
    def spec(j):
        return pl.BlockSpec((None, None, width, PAGE_SIZE), lambda b, g, pt: (layer, pt[b, g * group + j], 0, 0))

    return [spec(j) for _ in range(n) for j in range(group)]


def _rows_of_queries(k4, t_new):
    return jnp.concatenate([jnp.broadcast_to(k4[t:t + 1, :], (8, k4.shape[1])) for t in range(t_new)], axis=0)


def _select_topk_rows(keys_ref, lp, n_keep, pos_bits):
    cw = 16 * LANES

    def count(pred):
        acc = jnp.zeros((8, LANES), I32)
        for c0 in range(0, lp, cw):
            w = min(cw, lp - c0)
            hit = jnp.where(pred(keys_ref[0, :, c0:c0 + w], c0, w), 1, 0)
            for j in range(w // LANES):
                acc = acc + hit[:, j * LANES:(j + 1) * LANES]
        return jnp.sum(acc.astype(F32), axis=1, keepdims=True).astype(I32)

    def kpos(c0, w):
        return c0 + lax.broadcasted_iota(I32, (8, w), 1)

    def bit_step(i, tu):
        cand = tu | jnp.left_shift(jnp.int32(1), 31 - i)
        cnt = count(lambda k, c0, w: k >= (cand ^ INT_MIN))
        return jnp.where(cnt >= n_keep, cand, tu)

    thr = lax.fori_loop(0, 32, bit_step, jnp.zeros((8, 1), I32)) ^ INT_MIN
    excess = jnp.where(thr > INT_MIN, count(lambda k, c0, w: k >= thr) - n_keep, 0)

    @pl.when(jnp.max(excess) > 0)
    def _():
        need = n_keep - count(lambda k, c0, w: k > thr)

        def pos_step(i, c):
            cand = c | jnp.left_shift(jnp.int32(1), pos_bits - 1 - i)
            before = count(lambda k, c0, w: (k == thr) & (kpos(c0, w) < cand))
            return jnp.where(before <= need - 1, cand, c)

        cut = lax.fori_loop(0, pos_bits, pos_step, jnp.zeros((8, 1), I32))
        for c0 in range(0, lp, cw):
            w = min(cw, lp - c0)
            k = keys_ref[0, :, c0:c0 + w]
            late_tie = (k == thr) & (kpos(c0, w) > cut) & (excess > 0)
            keys_ref[0, :, c0:c0 + w] = jnp.where(late_tie, INT_MIN, k)

    return jnp.maximum(thr, INT_MIN + 1)


def _sample_index_kernel(group, n_groups, t_new, n_keep, pos_bits, pt_ref, *refs):
    nsq = SAMPLE_SEQS_PER_STEP
    pages = [refs[u * group:(u + 1) * group] for u in range(nsq)]
    iq_ref, w_ref, iknew_ref, keys_ref, thr_ref = refs[nsq * group:]
    g = pl.program_id(1)
    gk = group * PAGE_SIZE
    past = n_groups * gk

    def scores(u, kmat_t):
        s = jnp.maximum(_dot(iq_ref[u], kmat_t), 0.0) * w_ref[u][:, 0:1]
        return jnp.concatenate([jnp.sum(s[8 * t:8 * t + 8, :], axis=0, keepdims=True) for t in range(t_new)], axis=0)

    kcats = [jnp.concatenate([p[...].astype(BF16) for p in pages[u]], axis=1) for u in range(nsq)]
    key8 = jnp.concatenate([_key_of(scores(u, kcats[u])) for u in range(nsq)], axis=0)
    for gg in range(n_groups):
        @pl.when(g == gg)
        def _(gg=gg):
            keys_ref[0, :, gg * gk:(gg + 1) * gk] = key8

    @pl.when(g == n_groups - 1)
    def _():
        kn = jnp.concatenate([_key_of(scores(u, iknew_ref[u])) for u in range(nsq)], axis=0)
        tok = lax.broadcasted_iota(I32, kn.shape, 0) % t_new
        lane = lax.broadcasted_iota(I32, kn.shape, 1)
        keys_ref[0, :, past:past + LANES] = jnp.where((lane <= tok) & (lane < t_new), kn, INT_MIN)
        thr = _select_topk_rows(keys_ref, past + LANES, n_keep, pos_bits)
        thr_ref[0] = jnp.broadcast_to(thr, (8, LANES))


def _sample_index(layer, cache_idx_k, page_table, iq_rows, w_rows, iknew, group, n_keep):
    bd, n_pages = page_table.shape
    n_groups = n_pages // group
    t_new = iq_rows.shape[1] // 8
    nsq = SAMPLE_SEQS_PER_STEP
    assert nsq * t_new == 8 and bd % nsq == 0
    past = n_pages * PAGE_SIZE
    lp = past + LANES

    def page_spec(u, j):
        return pl.BlockSpec((None, None, IDX_D, PAGE_SIZE),
                            lambda b, g, pt: (layer, pt[b * nsq + u, g * group + j], 0, 0))

    grid_spec = pltpu.PrefetchScalarGridSpec(
        num_scalar_prefetch=1,
        grid=(bd // nsq, n_groups),
        in_specs=[page_spec(u, j) for u in range(nsq) for j in range(group)] + [
            pl.BlockSpec((nsq, 8 * t_new, IDX_D), lambda b, g, pt: (b, 0, 0)),
            pl.BlockSpec((nsq, 8 * t_new, LANES), lambda b, g, pt: (b, 0, 0)),
            pl.BlockSpec((nsq, IDX_D, PAGE_SIZE), lambda b, g, pt: (b, 0, 0))],
        out_specs=[pl.BlockSpec((1, 8, lp), lambda b, g, pt: (b, 0, 0)),
                   pl.BlockSpec((1, 8, LANES), lambda b, g, pt: (b, 0, 0))],
    )
    return pl.pallas_call(
        functools.partial(_sample_index_kernel, group, n_groups, t_new, n_keep, lp.bit_length()),
        grid_spec=grid_spec,
        out_shape=[jax.ShapeDtypeStruct((bd // nsq, 8, lp), I32), jax.ShapeDtypeStruct((bd // nsq, 8, LANES), I32)],
        compiler_params=pltpu.CompilerParams(dimension_semantics=("arbitrary", "arbitrary"),
                                             vmem_limit_bytes=VMEM_LIMIT),
        name=f"sample_index_l{layer}",
    )(page_table, *([cache_idx_k] * (nsq * group)), iq_rows, w_rows, iknew)


def _softmax_step(s, valid, v_t, m_ref, l_ref, acc_ref):
    if valid is not None:
        s = jnp.where(valid, s, NEG_BIG)
    m_old = m_ref[:, 0:1]
    m_new = jnp.maximum(m_old, jnp.max(s, axis=1, keepdims=True))
    alpha = jnp.exp2(m_old - m_new)
    e = jnp.exp2(s - m_new)
    if valid is not None:
        e = jnp.where(valid, e, 0.0)
    l_new = alpha * l_ref[:, 0:1] + jnp.sum(e, axis=1, keepdims=True)
    acc_ref[...] = alpha * acc_ref[...] + _dot_nt(e.astype(BF16), v_t)
    m_ref[...] = jnp.broadcast_to(m_new, m_ref.shape)
    l_ref[...] = jnp.broadcast_to(l_new, l_ref.shape)


def _sample_attn_kernel(group, n_groups, t_new, lam_init, pt_ref, *refs):
    n = group
    dk_pages, dv_pages, ck_pages, cv_pages = refs[0:n], refs[n:2 * n], refs[2 * n:3 * n], refs[3 * n:4 * n]
    (keys_ref, keysn_ref, thr_ref, qd_ref, qc_ref, dkn_ref, dvn_ref, ckn_ref, cvn_ref, dl_ref,
     ob_ref, oc_ref, md_ref, ld_ref, ad_ref, mc_ref, lc_ref, ac_ref) = refs[4 * n:]
    g = pl.program_id(1)
    rows = 8 * t_new

    @pl.when(g == 0)
    def _():
        for m_ref, l_ref, a_ref in ((md_ref, ld_ref, ad_ref), (mc_ref, lc_ref, ac_ref)):
            m_ref[...] = jnp.full(m_ref.shape, NEG_BIG, F32)
            l_ref[...] = jnp.zeros(l_ref.shape, F32)
            a_ref[...] = jnp.zeros(a_ref.shape, F32)

    def cat(pages):
        return jnp.concatenate([p[...].astype(BF16) for p in pages], axis=1)

    slot = pl.program_id(0) % SAMPLE_SEQS_PER_STEP

    def own_rows(x8):
        out = x8[0:t_new]
        for u in range(1, SAMPLE_SEQS_PER_STEP):
            out = jnp.where(slot == u, x8[u * t_new:(u + 1) * t_new], out)
        return out

    thr = _rows_of_queries(own_rows(thr_ref[0]), t_new)[:, 0:1]
    qd = qd_ref[0]
    qc = qc_ref[0]
    sel = _rows_of_queries(own_rows(keys_ref[0]), t_new) >= thr
    _softmax_step(_dot(qd, cat(dk_pages)), sel, cat(dv_pages), md_ref, ld_ref, ad_ref)
    _softmax_step(_dot(qc, cat(ck_pages)), None, cat(cv_pages), mc_ref, lc_ref, ac_ref)

    @pl.when(g == n_groups - 1)
    def _():
        row = lax.broadcasted_iota(I32, (rows, LANES), 0)
        lane = lax.broadcasted_iota(I32, (rows, LANES), 1)
        causal = (lane <= row // 8) & (lane < t_new)
        seln = causal & (_rows_of_queries(own_rows(keysn_ref[0]), t_new) >= thr)
        _softmax_step(_dot(qd, dkn_ref[0]), seln, dvn_ref[0], md_ref, ld_ref, ad_ref)
        _softmax_step(_dot(qc, ckn_ref[0]), causal, cvn_ref[0], mc_ref, lc_ref, ac_ref)

        r = lax.broadcasted_iota(I32, (rows, B_W), 0) % 8
        ln = lax.broadcasted_iota(I32, (rows, B_W), 1)
        od = jnp.where(ln // B_HD == r, ad_ref[...] / ld_ref[:, 0:1], 0.0)
        lam = _diff_lambda(dl_ref[...], lam_init)
        coef = jnp.where(r % 2 == 0, 1.0, -lam)
        oc = jnp.where(ln // C_DV == r // 2, coef * (ac_ref[...] / lc_ref[:, 0:1]), 0.0)
        pad = jnp.zeros((8 - t_new, B_W), F32)
        ob_ref[0] = jnp.concatenate([jnp.sum(od[8 * t:8 * t + 8], axis=0, keepdims=True) for t in range(t_new)] + [pad],
                                    axis=0)
        oc_ref[0] = jnp.concatenate([jnp.sum(oc[8 * t:8 * t + 8], axis=0, keepdims=True) for t in range(t_new)] + [pad],
                                    axis=0)


def _sample_attn(layer, caches, page_table, keys, thr, qd, qc, new_kv, dl, group):
    bd, n_pages = page_table.shape
    n_groups = n_pages // group
    rows = qd.shape[1]
    t_new = rows // 8
    gk = group * PAGE_SIZE
    past = n_pages * PAGE_SIZE
    lam_init = 0.8 - 0.6 * math.exp(-0.3 * layer)

    def per_seq(shape):
        nd = len(shape)
        return pl.BlockSpec((1,) + shape, lambda b, g, pt: (b,) + (0,) * nd)

    grid_spec = pltpu.PrefetchScalarGridSpec(
        num_scalar_prefetch=1,
        grid=(bd, n_groups),
        in_specs=_page_specs(layer, 4, group, B_W) + [
            pl.BlockSpec((1, 8, gk), lambda b, g, pt: (b // SAMPLE_SEQS_PER_STEP, 0, g)),
            pl.BlockSpec((1, 8, LANES), lambda b, g, pt: (b // SAMPLE_SEQS_PER_STEP, 0, past // LANES)),
            pl.BlockSpec((1, 8, LANES), lambda b, g, pt: (b // SAMPLE_SEQS_PER_STEP, 0, 0)),
            per_seq((rows, B_W)), per_seq((rows, C_W)),
            per_seq((B_W, PAGE_SIZE)), per_seq((B_W, PAGE_SIZE)), per_seq((C_W, PAGE_SIZE)), per_seq((C_W, PAGE_SIZE)),
            pl.BlockSpec((4, C_QK), lambda b, g, pt: (0, 0))],
        out_specs=[per_seq((8, B_W)), per_seq((8, C_W))],
        scratch_shapes=[pltpu.VMEM((rows, LANES), F32), pltpu.VMEM((rows, LANES), F32), pltpu.VMEM((rows, B_W), F32),
                        pltpu.VMEM((rows, LANES), F32), pltpu.VMEM((rows, LANES), F32), pltpu.VMEM((rows, C_W), F32)],
    )
    pages = [c for c in caches for _ in range(group)]
    return pl.pallas_call(
        functools.partial(_sample_attn_kernel, group, n_groups, t_new, lam_init),
        grid_spec=grid_spec,
        out_shape=[jax.ShapeDtypeStruct((bd, 8, B_W), F32), jax.ShapeDtypeStruct((bd, 8, C_W), F32)],
        compiler_params=pltpu.CompilerParams(dimension_semantics=("arbitrary", "arbitrary"),
                                             vmem_limit_bytes=VMEM_LIMIT),
        name=f"sample_attn_l{layer}",
    )(page_table, *pages, keys, keys, thr, qd, qc, *new_kv, dl)


def _outproj_kernel(c_scale, x_ref, oa_ref, ob_ref, oc_ref, gate_ref, w_ref, ma_ref, mc_ref, na_ref, nc_ref, y_ref):
    ga = _head_rms(oa_ref[0], ma_ref, A_DK, na_ref[...]) * gate_ref[0, :, 0:A_W]
    gb = ob_ref[0] * gate_ref[0, :, A_W:A_W + B_W]
    gc = _head_rms(oc_ref[0], mc_ref, C_DV, nc_ref[...]) * c_scale * gate_ref[0, :, A_W + B_W:]
    y = x_ref[0] + _dot(ga.astype(BF16), w_ref[0:A_W, :])
    y = y + _dot(gb.astype(BF16), w_ref[A_W:A_W + B_W, :])
    y_ref[0] = y + _dot(gc.astype(BF16), w_ref[A_W + B_W:, :])


def _outproj(layer, x, oa, ob, oc, gate, w_out, onorm_w, subln_w, tm):
    bn, t, d = x.shape
    lam_init = 0.8 - 0.6 * math.exp(-0.3 * layer)

    def tok(width):
        return pl.BlockSpec((1, tm, width), lambda b, i: (b, i, 0))

    def const(shape):
        return pl.BlockSpec(shape, lambda b, i: (0, 0))

    return pl.pallas_call(
        functools.partial(_outproj_kernel, 1.0 - lam_init),
        grid=(bn, t // tm),
        in_specs=[tok(d), tok(A_W), tok(B_W), tok(C_W), tok(d), const((d, d)), const((A_W, A_W)), const((C_W, C_W)),
                  const((1, A_W)), const((1, C_W))],
        out_specs=tok(d),
        out_shape=jax.ShapeDtypeStruct((bn, t, d), F32),
        compiler_params=pltpu.CompilerParams(dimension_semantics=("arbitrary", "arbitrary"),
                                             vmem_limit_bytes=VMEM_LIMIT),
        name=f"outproj_l{layer}_t{t}",
    )(x, oa, ob, oc, gate, w_out.astype(BF16), _block_diag_ones(A_W, A_DK), _block_diag_ones(C_W, C_DV),
      jnp.tile(onorm_w, A_HEADS).reshape(1, A_W), jnp.tile(subln_w, C_HEADS).reshape(1, C_W))


def kernel(x_prompt, x_sample, cache_dsa_k, cache_dsa_v, cache_idx_k, cache_diff_k, cache_diff_v, state_hgrn,
           page_table, norm_w, w_in, w_out, hgrn_lb_logits, hgrn_onorm_w, dsa_qnorm_w, dsa_knorm_w,
           diff_qnorm_w, diff_knorm_w, diff_lambda, diff_subln_w):
    depth = norm_w.shape[0]
    bn, s, d = x_prompt.shape
    bd, t_new, _ = x_sample.shape
    n_pool = cache_dsa_k.shape[1]
    n_pages = page_table.shape[1]
    past = n_pages * PAGE_SIZE
    group = math.gcd(SAMPLE_PAGE_GROUP, n_pages)
    pos_p = jnp.arange(s, dtype=I32)
    pos_s = past + jnp.arange(bd * t_new, dtype=I32) % t_new
    caches = [_token_minor(c) for c in (cache_dsa_k, cache_dsa_v, cache_diff_k, cache_diff_v)]
    idx_cache = _token_minor(cache_idx_k)
    xp = x_prompt
    xs = x_sample.reshape(1, bd * t_new, d)
    outs = {k: [] for k in ('p_bk', 'p_bv', 'p_ik', 'p_ck', 'p_cv', 'p_st', 's_bk', 's_bv', 's_ik', 's_ck', 's_cv', 's_st')}

    def pad_rows(a, n):
        return jnp.pad(a, ((0, 0), (0, n - a.shape[1]), (0, 0)))

    def tok_major(a_t, *feat):
        nf = len(feat)
        a = a_t.reshape((a_t.shape[0],) + feat + (a_t.shape[2],))
        return jnp.transpose(a, (0, nf + 1) + tuple(range(1, nf + 1)))

    def new_page(a_t):
        a = jnp.transpose(a_t[0].reshape(a_t.shape[1], bd, t_new), (1, 0, 2)).astype(BF16)
        return jnp.pad(a, ((0, 0), (0, 0), (0, PAGE_SIZE - t_new)))

    prompt_kv = None
    for l in range(depth):
        w_pack = _pack_w_in(w_in[l])
        norms = (dsa_qnorm_w[l], dsa_knorm_w[l], diff_qnorm_w[l], diff_knorm_w[l])

        o = _inproj(l, xp, pos_p, norm_w[l], w_pack, hgrn_lb_logits, *norms, tm=PROMPT_TILE, depth=depth,
                    stacked=prompt_kv)
        prompt_kv = {n: o[n] for n in _STACKED_OUTPUTS}
        oa, st = _hgrn(o['hg'], jnp.zeros((bn, A_HEADS, A_DK, A_DK), F32), PROMPT_TILE, f"hgrn_prompt_l{l}")
        ob = _dsa_prompt(o, PROMPT_TILE)
        oc = _diff_prompt(l, o, diff_lambda[l], PROMPT_TILE)
        xp = _outproj(l, xp, oa, ob, oc, o['gate'], w_out[l], hgrn_onorm_w[l], diff_subln_w[l], min(512, s))
        outs['p_st'].append(st)

        o = _inproj(l, xs, pos_s, norm_w[l], w_pack, hgrn_lb_logits, *norms, tm=bd * t_new)
        per_seq = lambda a: a.reshape(bd, t_new, a.shape[-1])
        hg = pad_rows(per_seq(o['hg']), HGRN_CHUNK)
        oa, st = _hgrn(hg, state_hgrn[l], HGRN_CHUNK, f"hgrn_sample_l{l}")
        oa = oa[:, :t_new].reshape(1, bd * t_new, A_W)
        iq_rows = o['iqb'].reshape(bd, t_new * IDX_HEADS, IDX_D)
        w_rows = jnp.broadcast_to(jnp.swapaxes(o['iwt'][0], 0, 1).reshape(bd, t_new * IDX_HEADS, 1),
                                  (bd, t_new * IDX_HEADS, LANES))
        keys, thr = _sample_index(l, idx_cache, page_table, iq_rows, w_rows, new_page(o['ikt']),
                                  math.gcd(SAMPLE_INDEX_GROUP, n_pages), min(TOPK_MAX, (past + t_new) // 4))
        lane = jnp.arange(B_W)
        r8 = jnp.arange(8)
        d_mask = (lane[None, :] // B_HD == r8[:, None])
        c_mask = (lane[None, :] // C_QK == r8[:, None])
        qd = jnp.where(d_mask[None, None], per_seq(o['bqb'])[:, :, None, :], 0).reshape(bd, 8 * t_new, B_W)
        qc = jnp.where(c_mask[None, None], per_seq(o['cqb'])[:, :, None, :], 0).reshape(bd, 8 * t_new, C_W)
        new_kv = [new_page(o[k]) for k in ('bkt', 'bvt32', 'ckt', 'cvt32')]
        ob, oc = _sample_attn(l, caches, page_table, keys, thr, qd, qc, new_kv, diff_lambda[l], group)
        ob = ob[:, :t_new].reshape(1, bd * t_new, B_W)
        oc = oc[:, :t_new].reshape(1, bd * t_new, C_W)
        xs = _outproj(l, xs, oa, ob, oc, o['gate'], w_out[l], hgrn_onorm_w[l], diff_subln_w[l], bd * t_new)
        outs['s_bk'].append(tok_major(o['bkt'], B_HEADS, B_HD).reshape(bd, t_new, B_HEADS, B_HD))
        outs['s_bv'].append(tok_major(o['bvt32'], B_HEADS, B_HD).reshape(bd, t_new, B_HEADS, B_HD))
        outs['s_ik'].append(tok_major(o['ikt'], IDX_D).reshape(bd, t_new, IDX_D))
        outs['s_ck'].append(tok_major(o['ckt'], C_HEADS, 2, C_QK).reshape(bd, t_new, C_HEADS, 2, C_QK))
        outs['s_cv'].append(tok_major(o['cvt32'], C_HEADS, C_DV).reshape(bd, t_new, C_HEADS, C_DV))
        outs['s_st'].append(st)

    stk = {k: jnp.stack(v) for k, v in outs.items() if v}

    def layers_tok_major(name, *feat):
        a = prompt_kv[name]
        return tok_major(a.reshape(depth * bn, a.shape[2], s), *feat).reshape((depth, bn, s) + feat)

    return (xp, xs.reshape(bd, t_new, d),
            layers_tok_major('bkt', B_HEADS, B_HD), layers_tok_major('bvt32', B_HEADS, B_HD),
            layers_tok_major('ikt', IDX_D), layers_tok_major('ckt', C_HEADS, 2, C_QK),
            layers_tok_major('cvt32', C_HEADS, C_DV), stk['p_st'],
            stk['s_bk'], stk['s_bv'], stk['s_ik'], stk['s_ck'], stk['s_cv'], stk['s_st'])
```
